```python
import math
import jax, jax.numpy as jnp
from jax import lax
import numpy as np

D_MODEL = 1024
BATCH = 8
SEQ = 4096
DEPTH = 2

HEAD_DIM = 64
A_HEADS = 6
A_NOPE = 64
A_ROPE = 32
A_V = 64
Q_LORA = 768
KV_LORA = 256
ROPE_THETA = 10000.0
B_HEADS = 6
DILATED_PATTERNS = ((128, 1), (512, 4), (2048, 16))
C_HEADS = 4
IDX_HEADS = 8
IDX_DIM = 64
TOPK_MAX = 256
Q_BLOCK = 128

A_WIDTH = A_HEADS * A_V
B_WIDTH = B_HEADS * HEAD_DIM
C_WIDTH = C_HEADS * HEAD_DIM
MIX_WIDTH = A_WIDTH + B_WIDTH + C_WIDTH
N_ALIBI = B_HEADS + C_HEADS

SPLIT_SIZES = (Q_LORA, KV_LORA, A_ROPE,
               B_WIDTH, B_WIDTH, B_WIDTH,
               C_WIDTH, C_WIDTH, C_WIDTH,
               IDX_HEADS * IDX_DIM, IDX_DIM, IDX_HEADS,
               MIX_WIDTH)
IN_WIDTH = sum(SPLIT_SIZES)
SPLIT_POINTS = [int(v) for v in np.cumsum(SPLIT_SIZES)[:-1]]

ALPHA = (2 * DEPTH) ** 0.25
BETA = (8 * DEPTH) ** -0.25

kernel_name = "hybrid_mla_dilated_dsa_deepnorm"

F32 = jnp.float32


def layer_norm(x, g, b, eps=1e-5):
    xf = x.astype(F32)
    mu = jnp.mean(xf, axis=-1, keepdims=True)
    xc = xf - mu
    var = jnp.mean(xc * xc, axis=-1, keepdims=True)
    return (xc * lax.rsqrt(var + eps) * g.astype(F32) + b.astype(F32)).astype(x.dtype)


def rms_norm(x, g, eps=1e-6):
    xf = x.astype(F32)
    ms = jnp.mean(xf * xf, axis=-1, keepdims=True)
    return (xf * lax.rsqrt(ms + eps) * g.astype(F32)).astype(x.dtype)


def apply_rope(x, cos, sin):
    half = x.shape[-1] // 2
    x1, x2 = x[..., :half].astype(F32), x[..., half:].astype(F32)
    return jnp.concatenate([x1 * cos - x2 * sin, x1 * sin + x2 * cos], axis=-1).astype(x.dtype)


def alibi_slopes():
    return 2.0 ** (-8.0 * jnp.arange(1, N_ALIBI + 1, dtype=F32) / N_ALIBI)


def to_blocks(a):
    b, s = a.shape[0], a.shape[1]
    return a.reshape((b, s // Q_BLOCK, Q_BLOCK) + a.shape[2:]).swapaxes(0, 1)


def from_blocks(a):
    a = a.swapaxes(0, 1)
    return a.reshape((a.shape[0], a.shape[1] * a.shape[2]) + a.shape[3:])


def mla_attention(q_nope, q_rope, k_nope, k_rope, v):
    s_len = q_nope.shape[1]
    nb = s_len // Q_BLOCK
    scale = (A_NOPE + A_ROPE) ** -0.5
    key_pos = jnp.arange(s_len)

    def block(args):
        qn, qr, start = args
        qpos = start + jnp.arange(Q_BLOCK)
        s = (jnp.einsum('bqhd,bshd->bhqs', qn, k_nope, preferred_element_type=F32)
             + jnp.einsum('bqhr,bsr->bhqs', qr, k_rope, preferred_element_type=F32)) * scale
        s = jnp.where(key_pos[None, :] <= qpos[:, None], s, -jnp.inf)
        p = jax.nn.softmax(s, axis=-1)
        return jnp.einsum('bhqs,bshd->bqhd', p.astype(v.dtype), v)

    out = lax.map(block, (to_blocks(q_nope), to_blocks(q_rope), jnp.arange(nb) * Q_BLOCK))
    return from_blocks(out)


def dilated_attention(q, k, v, slopes):
    bsz, s_len, nh, dh = q.shape
    outs, lses = [], []
    for window, dil in DILATED_PATTERNS:
        n = window // dil
        L = s_len // dil
        nb = -(-L // n)
        Lp = nb * n

        def gather(a):
            a = a.reshape(bsz, L, dil, nh, dh).swapaxes(1, 2)
            a = jnp.pad(a, ((0, 0), (0, 0), (0, Lp - L), (0, 0), (0, 0)))
            return a.reshape(bsz, dil, nb, n, nh, dh)

        def band(a):
            prev = jnp.pad(a, ((0, 0), (0, 0), (1, 0), (0, 0), (0, 0), (0, 0)))[:, :, :-1]
            return jnp.concatenate([prev, a], axis=3)

        qb = gather(q)
        kb = band(gather(k))
        vb = band(gather(v))
        s = jnp.einsum('brnqhd,brnkhd->brnhqk', qb, kb, preferred_element_type=F32) * dh ** -0.5
        kj = jnp.arange(2 * n)
        step = jnp.arange(n)[:, None] + n - kj[None, :]
        valid = ((step >= 0) & (step <= n))[None] & ((jnp.arange(nb)[:, None, None] > 0) | (kj >= n)[None, None, :])
        s = s - slopes[:, None, None] * (step * dil).astype(F32)
        s = jnp.where(valid[:, None], s, -jnp.inf)
        m = jnp.max(s, axis=-1, keepdims=True)
        p = jnp.exp(s - m)
        den = jnp.sum(p, axis=-1, keepdims=True)
        o = jnp.einsum('brnhqk,brnkhd->brnqhd', (p / den).astype(v.dtype), vb)
        lse = (m + jnp.log(den))[..., 0].swapaxes(-1, -2)

        def ungather(a):
            tr = a.shape[4:]
            a = a.reshape((bsz, dil, Lp) + tr)[:, :, :L]
            return a.swapaxes(1, 2).reshape((bsz, s_len) + tr)

        outs.append(ungather(o))
        lses.append(ungather(lse))
    w = jax.nn.softmax(jnp.stack(lses, axis=0), axis=0)
    out = jnp.sum(w[..., None] * jnp.stack(outs, axis=0).astype(F32), axis=0)
    return out.astype(q.dtype)


def sparse_attention(q, k, v, q_idx, k_idx, w_idx, slopes):
    s_len, dh = q.shape[1], q.shape[3]
    topk = min(TOPK_MAX, s_len // 4)
    nb = s_len // Q_BLOCK
    key_pos = jnp.arange(s_len)
    take = jax.vmap(lambda a, i: a[i])

    def block(args):
        qb, qib, wb, start = args
        qpos = start + jnp.arange(Q_BLOCK)
        logits = jnp.einsum('bqhd,bsd->bqhs', qib, k_idx, preferred_element_type=F32) * IDX_DIM ** -0.5
        score = jnp.einsum('bqh,bqhs->bqs', wb.astype(F32) * IDX_HEADS ** -0.5, jax.nn.relu(logits))
        score = jnp.where(key_pos[None, None, :] <= qpos[None, :, None], score, -jnp.inf)
        _, idx = lax.top_k(score, topk)
        valid = idx <= qpos[None, :, None]
        kg = take(k, idx)
        vg = take(v, idx)
        s = jnp.einsum('bqhd,bqkhd->bhqk', qb, kg, preferred_element_type=F32) * dh ** -0.5
        dist = (qpos[None, :, None] - idx).astype(F32)
        s = s - slopes[None, :, None, None] * dist[:, None]
        s = jnp.where(valid[:, None], s, -jnp.inf)
        p = jax.nn.softmax(s, axis=-1)
        return jnp.einsum('bhqk,bqkhd->bqhd', p.astype(v.dtype), vg)

    out = lax.map(block, (to_blocks(q), to_blocks(q_idx), to_blocks(w_idx), jnp.arange(nb) * Q_BLOCK))
    return from_blocks(out)


def setup_inputs(seed: int = 0) -> dict:
    key = jax.random.key(seed)
    ks = jax.random.split(key, 14)
    nrm = jax.random.normal
    return {
        "x": nrm(ks[0], (BATCH, SEQ, D_MODEL), F32),
        "c": nrm(ks[1], (BATCH, D_MODEL), F32),
        "w_ada": nrm(ks[2], (DEPTH, D_MODEL, 3 * D_MODEL), F32) * (0.1 * D_MODEL ** -0.5),
        "b_ada": nrm(ks[3], (DEPTH, 3 * D_MODEL), F32) * 0.02,
        "w_in": nrm(ks[4], (DEPTH, D_MODEL, IN_WIDTH), F32) * D_MODEL ** -0.5,
        "q_norm_g": 1.0 + 0.02 * nrm(ks[5], (DEPTH, Q_LORA), F32),
        "kv_norm_g": 1.0 + 0.02 * nrm(ks[6], (DEPTH, KV_LORA), F32),
        "w_uq": nrm(ks[7], (DEPTH, Q_LORA, A_HEADS * (A_NOPE + A_ROPE)), F32) * Q_LORA ** -0.5,
        "w_uk": nrm(ks[8], (DEPTH, KV_LORA, A_HEADS * A_NOPE), F32) * KV_LORA ** -0.5,
        "w_uv": nrm(ks[9], (DEPTH, KV_LORA, A_HEADS * A_V), F32) * KV_LORA ** -0.5,
        "w_out": nrm(ks[10], (DEPTH, MIX_WIDTH, D_MODEL), F32) * (BETA * MIX_WIDTH ** -0.5),
        "ln_g": 1.0 + 0.02 * nrm(ks[11], (DEPTH, D_MODEL), F32),
        "ln_b": 0.02 * nrm(ks[12], (DEPTH, D_MODEL), F32),
    }


def reference(x, c, w_ada, b_ada, w_in, q_norm_g, kv_norm_g, w_uq, w_uk, w_uv, w_out, ln_g, ln_b):
    bsz, s_len, _ = x.shape
    slopes = alibi_slopes()
    slopes_b, slopes_c = slopes[:B_HEADS], slopes[B_HEADS:]
    pos = jnp.arange(s_len, dtype=F32)
    freqs = ROPE_THETA ** (-jnp.arange(0, A_ROPE, 2, dtype=F32) / A_ROPE)
    ang = pos[:, None] * freqs[None, :]
    cos, sin = jnp.cos(ang), jnp.sin(ang)

    def heads(a, n):
        return a.reshape(bsz, s_len, n, -1)

    for l in range(DEPTH):
        mod = jax.nn.silu(c) @ w_ada[l] + b_ada[l]
        shift, scale, gate = jnp.split(mod, 3, axis=-1)
        h = x * (1.0 + scale[:, None, :]) + shift[:, None, :]
        (p_cq, p_ckv, p_kr, p_qb, p_kb, p_vb, p_qc, p_kc, p_vc,
         p_qi, p_ki, p_wi, p_gate) = jnp.split(h @ w_in[l], SPLIT_POINTS, axis=-1)

        cq = rms_norm(p_cq, q_norm_g[l])
        ckv = rms_norm(p_ckv, kv_norm_g[l])
        qa = heads(cq @ w_uq[l], A_HEADS)
        qa_nope = qa[..., :A_NOPE]
        qa_rope = apply_rope(qa[..., A_NOPE:], cos[:, None, :], sin[:, None, :])
        ka_nope = heads(ckv @ w_uk[l], A_HEADS)
        va = heads(ckv @ w_uv[l], A_HEADS)
        ka_rope = apply_rope(p_kr, cos, sin)
        o_a = mla_attention(qa_nope, qa_rope, ka_nope, ka_rope, va)

        o_b = dilated_attention(heads(p_qb, B_HEADS), heads(p_kb, B_HEADS), heads(p_vb, B_HEADS), slopes_b)

        o_c = sparse_attention(heads(p_qc, C_HEADS), heads(p_kc, C_HEADS), heads(p_vc, C_HEADS),
                               heads(p_qi, IDX_HEADS), p_ki, p_wi, slopes_c)

        y = jnp.concatenate([o_a.reshape(bsz, s_len, A_WIDTH),
                             o_b.reshape(bsz, s_len, B_WIDTH),
                             o_c.reshape(bsz, s_len, C_WIDTH)], axis=-1) * jax.nn.silu(p_gate)
        sub = y @ w_out[l]
        x = layer_norm(ALPHA * x + (1.0 + gate[:, None, :]) * sub, ln_g[l], ln_b[l])
    return x
```

```python
import functools

import numpy as np
import jax
import jax.numpy as jnp
from jax import lax
from jax.experimental import pallas as pl
from jax.experimental.pallas import tpu as pltpu

F32 = jnp.float32
BF16 = jnp.bfloat16
I32 = jnp.int32

D_MODEL = 1024
HEAD_DIM = 64
A_HEADS = 6
A_NOPE = 64
A_ROPE = 32
A_V = 64
Q_LORA = 768
KV_LORA = 256
ROPE_THETA = 10000.0
B_HEADS = 6
DILATED_PATTERNS = ((128, 1), (512, 4), (2048, 16))
C_HEADS = 4
IDX_HEADS = 8
IDX_DIM = 64
TOPK_MAX = 256
A_WIDTH = A_HEADS * A_V
B_WIDTH = B_HEADS * HEAD_DIM
C_WIDTH = C_HEADS * HEAD_DIM
MIX_WIDTH = A_WIDTH + B_WIDTH + C_WIDTH
N_ALIBI = B_HEADS + C_HEADS
SPLIT_SIZES = (Q_LORA, KV_LORA, A_ROPE, B_WIDTH, B_WIDTH, B_WIDTH, C_WIDTH, C_WIDTH, C_WIDTH,
               IDX_HEADS * IDX_DIM, IDX_DIM, IDX_HEADS, MIX_WIDTH)
_OFFS = [0] + [int(v) for v in np.cumsum(SPLIT_SIZES)]
(O_CQ, O_CKV, O_KR, O_QB, O_KB, O_VB, O_QC, O_KC, O_VC, O_QI, O_KI, O_WI, O_GATE, _O_END) = _OFFS

LANES = 128
MASKED = -1e30
VMEM_LIMIT = 48 * 1024 * 1024

PROJ_ROWS = 512
MLA_TQ = 512
DIL_N = 128
DIL_TQ = 512
DSA_Q = 128
DSA_CH = 512
KEY_ALL_FINITE = int(np.int32(np.uint32(0xFF800000) ^ np.uint32(0x7FFFFFFF))) + 1


def _dot(a, b):
    return jnp.dot(a, b, preferred_element_type=F32)


def _dot_nt(a, b):
    return lax.dot_general(a, b, (((1,), (1,)), ((), ())), preferred_element_type=F32)


def _cparams(sem):
    return pltpu.CompilerParams(dimension_semantics=sem, vmem_limit_bytes=VMEM_LIMIT)


def _mod_kernel(c_ref, w_ref, b_ref, o_ref):
    c = c_ref[...]
    sc = (c * jax.nn.sigmoid(c)).astype(BF16)
    o_ref[0] = _dot(sc, w_ref[0].astype(BF16)) + b_ref[0]


def _mod_call(c, w_ada, b_ada):
    depth, d, n3 = w_ada.shape
    bsz = c.shape[0]
    tn = 1024
    return pl.pallas_call(
        _mod_kernel,
        grid=(depth, n3 // tn),
        in_specs=[pl.BlockSpec((bsz, d), lambda l, j: (0, 0)),
                  pl.BlockSpec((1, d, tn), lambda l, j: (l, 0, j)),
                  pl.BlockSpec((1, 1, tn), lambda l, j: (l, 0, j))],
        out_specs=pl.BlockSpec((1, bsz, tn), lambda l, j: (l, 0, j)),
        out_shape=jax.ShapeDtypeStruct((depth, bsz, n3), F32),
        compiler_params=_cparams(("arbitrary", "arbitrary")),
    )(c, w_ada, b_ada.reshape(depth, 1, n3))


def _modulate(x_ref, mod_ref):
    mod = mod_ref[...]
    shift = mod[:, :D_MODEL]
    scale = mod[:, D_MODEL:2 * D_MODEL]
    return (x_ref[...] * (1.0 + scale) + shift).astype(BF16)


def _rms(x, g, eps=1e-6):
    ms = jnp.mean(x * x, axis=-1, keepdims=True)
    return (x * lax.rsqrt(ms + eps) * g).astype(BF16)


def _proj_mla_kernel(x_ref, mod_ref, w1_ref, gq_ref, gkv_ref, wq1_ref, wq2_ref, wk_ref, wv_ref,
                     ct_ref, st_ref, q_ref, k_ref, v_ref):
    h = _modulate(x_ref, mod_ref)
    pa = _dot(h, w1_ref[...])
    cq = _rms(pa[:, :Q_LORA], gq_ref[...])
    ckv = _rms(pa[:, Q_LORA:Q_LORA + KV_LORA], gkv_ref[...])
    ct = ct_ref[...]
    st = st_ref[...]
    ct6 = jnp.concatenate([ct] * A_HEADS, axis=1)
    st6 = jnp.concatenate([st] * A_HEADS, axis=1)
    qscale = (A_NOPE + A_ROPE) ** -0.5
    q = (_dot(cq, wq1_ref[...]) * ct6 + _dot(cq, wq2_ref[...]) * st6) * qscale
    q_ref[...] = q.astype(BF16)
    o = Q_LORA + KV_LORA
    kr = pa[:, o:o + LANES] * ct + pa[:, o + LANES:o + 2 * LANES] * st
    k = _dot(ckv, wk_ref[...]) + jnp.concatenate([kr] * A_HEADS, axis=1)
    k_ref[...] = k.astype(BF16)
    v_ref[...] = _dot(ckv, wv_ref[...]).astype(BF16)


def _proj_mla_call(x, mod, w1, gq, gkv, wq1, wq2, wk, wv, ct, st):
    bsz, s_len, d = x.shape
    tm = PROJ_ROWS
    hw = A_HEADS * LANES
    full = lambda a: pl.BlockSpec(a.shape, lambda b, i: (0,) * a.ndim)
    return pl.pallas_call(
        _proj_mla_kernel,
        grid=(bsz, s_len // tm),
        in_specs=[pl.BlockSpec((None, tm, d), lambda b, i: (b, i, 0)),
                  pl.BlockSpec((None, 1, 3 * d), lambda b, i: (b, 0, 0)),
                  full(w1), full(gq), full(gkv), full(wq1), full(wq2), full(wk), full(wv),
                  pl.BlockSpec((tm, LANES), lambda b, i: (i, 0)),
                  pl.BlockSpec((tm, LANES), lambda b, i: (i, 0))],
        out_specs=[pl.BlockSpec((None, tm, hw), lambda b, i: (b, i, 0)),
                   pl.BlockSpec((None, tm, hw), lambda b, i: (b, i, 0)),
                   pl.BlockSpec((None, tm, A_WIDTH), lambda b, i: (b, i, 0))],
        out_shape=[jax.ShapeDtypeStruct((bsz, s_len, hw), BF16),
                   jax.ShapeDtypeStruct((bsz, s_len, hw), BF16),
                   jax.ShapeDtypeStruct((bsz, s_len, A_WIDTH), BF16)],
        compiler_params=_cparams(("parallel", "parallel")),
    )(x, mod, w1, gq, gkv, wq1, wq2, wk, wv, ct, st)


_R_QB, _R_KB, _R_VB = 0, B_WIDTH, 2 * B_WIDTH
_R_QC = 3 * B_WIDTH
_R_KC, _R_VC = _R_QC + C_WIDTH, _R_QC + 2 * C_WIDTH
_R_QI = _R_QC + 3 * C_WIDTH
_R_KI = _R_QI + IDX_HEADS * IDX_DIM
_R_WI = _R_KI + 2 * LANES
_R_GATE = _R_WI + LANES
_R_END = _R_GATE + MIX_WIDTH


def _proj_rest_kernel(x_ref, mod_ref, w2_ref, qb_ref, kb_ref, vb_ref, qc_ref, kc_ref, vc_ref,
                      qi_ref, ki_ref, wi_ref, gate_ref):
    h = _modulate(x_ref, mod_ref)

    def proj(start, width):
        return _dot(h, w2_ref[:, start:start + width])

    inv_sqrt_dh = HEAD_DIM ** -0.5
    qb_ref[...] = (proj(_R_QB, B_WIDTH) * inv_sqrt_dh).astype(BF16)
    kb_ref[...] = proj(_R_KB, B_WIDTH).astype(BF16)
    vb_ref[...] = proj(_R_VB, B_WIDTH).astype(BF16)
    qc_ref[...] = (proj(_R_QC, C_WIDTH) * inv_sqrt_dh).astype(BF16)
    kc_ref[...] = proj(_R_KC, C_WIDTH).astype(BF16)
    vc_ref[...] = proj(_R_VC, C_WIDTH).astype(BF16)
    qi_ref[...] = (proj(_R_QI, IDX_HEADS * IDX_DIM) * IDX_DIM ** -0.5).astype(BF16)
    ki_ref[...] = proj(_R_KI, 2 * LANES).astype(BF16)
    wi_ref[...] = proj(_R_WI, LANES) * IDX_HEADS ** -0.5
    gate_ref[...] = proj(_R_GATE, MIX_WIDTH)


def _proj_rest_call(x, mod, w2):
    bsz, s_len, d = x.shape
    tm = PROJ_ROWS
    widths = (B_WIDTH, B_WIDTH, B_WIDTH, C_WIDTH, C_WIDTH, C_WIDTH, IDX_HEADS * IDX_DIM,
              2 * LANES, LANES, MIX_WIDTH)
    dtypes = (BF16,) * 8 + (F32, F32)
    return pl.pallas_call(
        _proj_rest_kernel,
        grid=(bsz, s_len // tm),
        in_specs=[pl.BlockSpec((None, tm, d), lambda b, i: (b, i, 0)),
                  pl.BlockSpec((None, 1, 3 * d), lambda b, i: (b, 0, 0)),
                  pl.BlockSpec(w2.shape, lambda b, i: (0, 0))],
        out_specs=[pl.BlockSpec((None, tm, w), lambda b, i: (b, i, 0)) for w in widths],
        out_shape=[jax.ShapeDtypeStruct((bsz, s_len, w), dt) for w, dt in zip(widths, dtypes)],
        compiler_params=_cparams(("parallel", "parallel")),
    )(x, mod, w2)


def _online_update(s, v, m_ref, l_ref, acc_ref, idx):
    m_prev = m_ref[idx][:, :1]
    l_prev = l_ref[idx][:, :1]
    m_new = jnp.maximum(m_prev, jnp.max(s, axis=1, keepdims=True))
    alpha = jnp.exp(m_prev - m_new)
    p = jnp.exp(s - m_new)
    l_new = alpha * l_prev + jnp.sum(p, axis=1, keepdims=True)
    acc_ref[idx] = alpha * acc_ref[idx] + _dot(p.astype(BF16), v)
    m_ref[idx] = jnp.broadcast_to(m_new, m_ref.shape[1:])
    l_ref[idx] = jnp.broadcast_to(l_new, l_ref.shape[1:])


def _init_state(m_ref, l_ref, acc_ref):
    m_ref[...] = jnp.full(m_ref.shape, MASKED, F32)
    l_ref[...] = jnp.zeros(l_ref.shape, F32)
    acc_ref[...] = jnp.zeros(acc_ref.shape, F32)


def _pair_output(l_ref, acc_ref, i0, i1):
    lane = lax.broadcasted_iota(I32, acc_ref.shape[1:], 1)
    o0 = acc_ref[i0] / l_ref[i0][:, :1]
    o1 = acc_ref[i1] / l_ref[i1][:, :1]
    return jnp.where(lane < HEAD_DIM, o0, o1)


def _mla_kernel(q_ref, k_ref, v_ref, o_ref, m_ref, l_ref, acc_ref):
    qi = pl.program_id(2)
    tq = q_ref.shape[0]
    _init_state(m_ref, l_ref, acc_ref)

    def scores(hh, c):
        q = q_ref[:, hh * LANES:(hh + 1) * LANES]
        k = k_ref[pl.ds(pl.multiple_of(c * tq, tq), tq), hh * LANES:(hh + 1) * LANES]
        return _dot_nt(q, k)

    def body(c, carry):
        v = v_ref[pl.ds(pl.multiple_of(c * tq, tq), tq), :]
        for hh in range(2):
            _online_update(scores(hh, c), v, m_ref, l_ref, acc_ref, hh)
        return carry

    lax.fori_loop(0, qi, body, 0)
    row = lax.broadcasted_iota(I32, (tq, tq), 0)
    col = lax.broadcasted_iota(I32, (tq, tq), 1)
    v = v_ref[pl.ds(pl.multiple_of(qi * tq, tq), tq), :]
    for hh in range(2):
        s = jnp.where(col <= row, scores(hh, qi), MASKED)
        _online_update(s, v, m_ref, l_ref, acc_ref, hh)
    o_ref[...] = _pair_output(l_ref, acc_ref, 0, 1)


def _mla_call(q, k, v):
    bsz, s_len, _ = q.shape
    tq = MLA_TQ
    npair = A_HEADS // 2
    return pl.pallas_call(
        _mla_kernel,
        grid=(bsz, npair, s_len // tq),
        in_specs=[pl.BlockSpec((None, tq, 2 * LANES), lambda b, p, i: (b, i, p)),
                  pl.BlockSpec((None, s_len, 2 * LANES), lambda b, p, i: (b, 0, p)),
                  pl.BlockSpec((None, s_len, LANES), lambda b, p, i: (b, 0, p))],
        out_specs=pl.BlockSpec((None, tq, LANES), lambda b, p, i: (b, i, p)),
        out_shape=jax.ShapeDtypeStruct((bsz, s_len, A_WIDTH), F32),
        scratch_shapes=[pltpu.VMEM((2, tq, LANES), F32)] * 3,
        compiler_params=_cparams(("parallel", "parallel", "arbitrary")),
    )(q, k, v)


def _dil_kernel(q_ref, k_ref, v_ref, bias_ref, o_ref, lse_ref):
    i = pl.program_id(2)
    n = DIL_N
    blocks = q_ref.shape[0] // n
    lane = lax.broadcasted_iota(I32, (n, LANES), 1)
    for blk in range(blocks):
        j = i * blocks + blk
        cur = pl.multiple_of(j * n, n)
        prev = pl.multiple_of(jnp.maximum(j - 1, 0) * n, n)
        has_prev = j > 0
        for p in range(B_HEADS // 2):
            cols = slice(p * LANES, (p + 1) * LANES)
            q = q_ref[blk * n:(blk + 1) * n, cols]
            k_prev = k_ref[pl.ds(prev, n), cols]
            k_cur = k_ref[pl.ds(cur, n), cols]
            v_prev = v_ref[pl.ds(prev, n), cols]
            v_cur = v_ref[pl.ds(cur, n), cols]
            outs, lses = [], []
            for hh in range(2):
                h = 2 * p + hh
                own = (lane >= hh * HEAD_DIM) & (lane < (hh + 1) * HEAD_DIM)
                zero = jnp.zeros_like(k_cur)
                s_prev = _dot_nt(q, jnp.where(own, k_prev, zero)) + bias_ref[h, :, :n]
                s_prev = jnp.where(has_prev, s_prev, MASKED)
                s_cur = _dot_nt(q, jnp.where(own, k_cur, zero)) + bias_ref[h, :, n:]
                m = jnp.maximum(jnp.max(s_prev, axis=1, keepdims=True),
                                jnp.max(s_cur, axis=1, keepdims=True))
                p_prev = jnp.exp(s_prev - m)
                p_cur = jnp.exp(s_cur - m)
                den = jnp.sum(p_prev, axis=1, keepdims=True) + jnp.sum(p_cur, axis=1, keepdims=True)
                o = (_dot(p_prev.astype(BF16), v_prev) + _dot(p_cur.astype(BF16), v_cur)) / den
                outs.append(o)
                lses.append(jnp.broadcast_to(m + jnp.log(den), (n, LANES)))
            first = lane < HEAD_DIM
            o_ref[blk * n:(blk + 1) * n, cols] = jnp.where(first, outs[0], outs[1])
            lse_ref[blk * n:(blk + 1) * n, cols] = jnp.where(first, lses[0], lses[1])


def _dil_bias(dil, slopes_b):
    n = DIL_N
    kj = jnp.arange(2 * n)
    step = jnp.arange(n)[:, None] + n - kj[None, :]
    valid = (step >= 0) & (step <= n)
    bias = -slopes_b[:, None, None] * (step * dil).astype(F32)[None]
    return jnp.where(valid[None], bias, MASKED)


def _dil_call(q, k, v, dil, slopes_b):
    bsz, s_len, w = q.shape
    L = s_len // dil
    tq = min(DIL_TQ, L)
    view = lambda a: a.reshape(bsz, L, dil * w)
    bias = _dil_bias(dil, slopes_b)
    o, lse = pl.pallas_call(
        _dil_kernel,
        grid=(bsz, dil, L // tq),
        in_specs=[pl.BlockSpec((None, tq, w), lambda b, r, i: (b, i, r)),
                  pl.BlockSpec((None, L, w), lambda b, r, i: (b, 0, r)),
                  pl.BlockSpec((None, L, w), lambda b, r, i: (b, 0, r)),
                  pl.BlockSpec(bias.shape, lambda b, r, i: (0, 0, 0))],
        out_specs=[pl.BlockSpec((None, tq, w), lambda b, r, i: (b, i, r))] * 2,
        out_shape=[jax.ShapeDtypeStruct((bsz, L, dil * w), F32)] * 2,
        compiler_params=_cparams(("parallel", "parallel", "arbitrary")),
    )(view(q), view(k), view(v), bias)
    return o.reshape(bsz, s_len, w), lse.reshape(bsz, s_len, w)


def _sortable(x):
    b = lax.bitcast_convert_type(x, I32)
    return b ^ ((b >> 31) & 0x7FFFFFFF)


def _dsa_kernel(qi_ref, ki_ref, wi_ref, qc_ref, kc_ref, vc_ref, tri_ref, slope_ref, o_ref,
                key_ref, sel_ref, thr_ref, m_ref, l_ref, acc_ref):
    qb = pl.program_id(1)
    nq, ch = DSA_Q, DSA_CH
    q0 = qb * nq
    nk = (q0 + nq + ch - 1) // ch
    row = lax.broadcasted_iota(I32, (nq, ch), 0)
    col = lax.broadcasted_iota(I32, (nq, ch), 1)
    rel = col - row

    w = wi_ref[...]

    def score_body(c, carry):
        ks = pl.ds(pl.multiple_of(c * ch, ch), ch)
        k_lo = ki_ref[ks, :LANES]
        k_hi = ki_ref[ks, LANES:]
        sc = jnp.zeros((nq, ch), F32)
        for j in range(IDX_HEADS // 2):
            qj = qi_ref[:, j * LANES:(j + 1) * LANES]
            sc = sc + w[:, 2 * j:2 * j + 1] * jnp.maximum(_dot_nt(qj, k_lo), 0.0)
            sc = sc + w[:, 2 * j + 1:2 * j + 2] * jnp.maximum(_dot_nt(qj, k_hi), 0.0)
        causal = rel <= q0 - c * ch
        key_ref[c] = _sortable(jnp.where(causal, sc, -jnp.inf))
        return carry

    lax.fori_loop(0, nk, score_body, 0)

    def count(pred):
        def body(c, acc):
            hit = jnp.where(pred(key_ref[c]), 1.0, 0.0)
            part = hit[:, :LANES]
            for t in range(1, ch // LANES):
                part = part + hit[:, t * LANES:(t + 1) * LANES]
            return acc + part
        acc = lax.fori_loop(0, nk, body, jnp.zeros((nq, LANES), F32))
        return jnp.sum(acc, axis=1, keepdims=True)

    topk = float(TOPK_MAX)
    thr_ref[...] = jnp.full(thr_ref.shape, KEY_ALL_FINITE, I32)
    need_tie = jnp.zeros((), jnp.bool_)

    @pl.when(q0 + 1 > TOPK_MAX)
    def _():
        zero = jnp.zeros((nq, 1), I32)
        r0 = jnp.where(count(lambda kk: kk >= zero) >= topk, zero, jnp.full((nq, 1), -2 ** 31, I32))

        def bit_body(it, r):
            cand = r + jnp.left_shift(jnp.int32(1), 30 - it)
            return jnp.where(count(lambda kk: kk >= cand) >= topk, cand, r)

        thr_ref[...] = jnp.broadcast_to(lax.fori_loop(0, 31, bit_body, r0), thr_ref.shape)

    thr = thr_ref[:, :1]
    n_ge = count(lambda kk: kk >= thr)
    has_tie = jnp.logical_and(q0 + 1 > TOPK_MAX, jnp.max(n_ge) > topk)

    @pl.when(jnp.logical_not(has_tie))
    def _():
        def body(c, carry):
            sel_ref[c] = jnp.where(key_ref[c] >= thr, 0.0, MASKED)
            return carry
        lax.fori_loop(0, nk, body, 0)

    @pl.when(has_tie)
    def _():
        need = topk - count(lambda kk: kk > thr)

        def body(c, seen):
            kk = key_ref[c]
            eq = kk == thr
            rank = seen + _dot(jnp.where(eq, 1.0, 0.0).astype(BF16), tri_ref[...])
            take = (kk > thr) | (eq & (rank <= need))
            sel_ref[c] = jnp.where(take, 0.0, MASKED)
            return rank[:, ch - 1:ch]
        lax.fori_loop(0, nk, body, jnp.zeros((nq, 1), F32))

    _init_state(m_ref, l_ref, acc_ref)
    lane = lax.broadcasted_iota(I32, (ch, LANES), 1)
    relf = rel.astype(F32)

    def attn_body(c, carry):
        ks = pl.ds(pl.multiple_of(c * ch, ch), ch)
        dist0 = (c * ch - q0).astype(F32)
        sel = sel_ref[c]
        for p in range(C_HEADS // 2):
            cols = slice(p * LANES, (p + 1) * LANES)
            q = qc_ref[:, cols]
            k = kc_ref[ks, cols]
            v = vc_ref[ks, cols]
            for hh in range(2):
                h = 2 * p + hh
                own = (lane >= hh * HEAD_DIM) & (lane < (hh + 1) * HEAD_DIM)
                s = _dot_nt(q, jnp.where(own, k, jnp.zeros_like(k)))
                s = s + slope_ref[h] * (relf + dist0) + sel
                _online_update(s, v, m_ref, l_ref, acc_ref, h)
        return carry

    lax.fori_loop(0, nk, attn_body, 0)
    for p in range(C_HEADS // 2):
        o_ref[:, p * LANES:(p + 1) * LANES] = _pair_output(l_ref, acc_ref, 2 * p, 2 * p + 1)


def _dsa_call(qi, ki, wi, qc, kc, vc, slopes_c):
    bsz, s_len, _ = qc.shape
    nq, ch = DSA_Q, DSA_CH
    tri = (jnp.arange(ch)[:, None] <= jnp.arange(ch)[None, :]).astype(BF16)
    blk = lambda w: pl.BlockSpec((None, nq, w), lambda b, i: (b, i, 0))
    seq = lambda w: pl.BlockSpec((None, s_len, w), lambda b, i: (b, 0, 0))
    return pl.pallas_call(
        _dsa_kernel,
        grid=(bsz, s_len // nq),
        in_specs=[blk(IDX_HEADS * IDX_DIM), seq(2 * LANES), blk(LANES),
                  blk(C_WIDTH), seq(C_WIDTH), seq(C_WIDTH),
                  pl.BlockSpec((ch, ch), lambda b, i: (0, 0)),
                  pl.BlockSpec(memory_space=pltpu.SMEM)],
        out_specs=blk(C_WIDTH),
        out_shape=jax.ShapeDtypeStruct((bsz, s_len, C_WIDTH), F32),
        scratch_shapes=[pltpu.VMEM((s_len // ch, nq, ch), I32),
                        pltpu.VMEM((s_len // ch, nq, ch), F32),
                        pltpu.VMEM((nq, LANES), I32),
                        pltpu.VMEM((C_HEADS, nq, LANES), F32),
                        pltpu.VMEM((C_HEADS, nq, LANES), F32),
                        pltpu.VMEM((C_HEADS, nq, LANES), F32)],
        compiler_params=_cparams(("parallel", "arbitrary")),
    )(qi, ki, wi, qc, kc, vc, tri, slopes_c)


def _out_kernel(alpha, x_ref, mod_ref, oa_ref, ob1_ref, ob2_ref, ob3_ref, l1_ref, l2_ref, l3_ref,
                oc_ref, gate_ref, wo_ref, g_ref, b_ref, y_ref):
    l1, l2, l3 = l1_ref[...], l2_ref[...], l3_ref[...]
    mx = jnp.maximum(jnp.maximum(l1, l2), l3)
    e1, e2, e3 = jnp.exp(l1 - mx), jnp.exp(l2 - mx), jnp.exp(l3 - mx)
    ob = (e1 * ob1_ref[...] + e2 * ob2_ref[...] + e3 * ob3_ref[...]) / (e1 + e2 + e3)
    pg = gate_ref[...]
    sg = pg * jax.nn.sigmoid(pg)
    ya = (oa_ref[...] * sg[:, :A_WIDTH]).astype(BF16)
    yb = (ob * sg[:, A_WIDTH:A_WIDTH + B_WIDTH]).astype(BF16)
    yc = (oc_ref[...] * sg[:, A_WIDTH + B_WIDTH:]).astype(BF16)
    sub = (_dot(ya, wo_ref[:A_WIDTH, :]) + _dot(yb, wo_ref[A_WIDTH:A_WIDTH + B_WIDTH, :])
           + _dot(yc, wo_ref[A_WIDTH + B_WIDTH:, :]))
    gate = mod_ref[...][:, 2 * D_MODEL:]
    z = alpha * x_ref[...] + (1.0 + gate) * sub
    mu = jnp.mean(z, axis=-1, keepdims=True)
    zc = z - mu
    var = jnp.mean(zc * zc, axis=-1, keepdims=True)
    y_ref[...] = zc * lax.rsqrt(var + 1e-5) * g_ref[...] + b_ref[...]


def _out_call(alpha, x, mod, oa, obs, lses, oc, gate, wo, g, b):
    bsz, s_len, d = x.shape
    tm = PROJ_ROWS
    rows = lambda w: pl.BlockSpec((None, tm, w), lambda bb, i: (bb, i, 0))
    full = lambda a: pl.BlockSpec(a.shape, lambda bb, i: (0,) * a.ndim)
    return pl.pallas_call(
        functools.partial(_out_kernel, alpha),
        grid=(bsz, s_len // tm),
        in_specs=[rows(d), pl.BlockSpec((None, 1, 3 * d), lambda bb, i: (bb, 0, 0)),
                  rows(A_WIDTH)] + [rows(B_WIDTH)] * 6 + [rows(C_WIDTH), rows(MIX_WIDTH),
                  full(wo), full(g), full(b)],
        out_specs=rows(d),
        out_shape=jax.ShapeDtypeStruct((bsz, s_len, d), F32),
        compiler_params=_cparams(("parallel", "parallel")),
    )(x, mod, oa, *obs, *lses, oc, gate, wo, g, b)


def _pad_cols(a, width):
    return jnp.pad(a, ((0, 0), (0, width - a.shape[1])))


def _swap_halves(a):
    half = a.shape[1] // 2
    return jnp.concatenate([a[:, half:], a[:, :half]], axis=1)


def _layer_weights(w_in, w_uq, w_uk, w_uv, w_out):
    d = w_in.shape[0]
    z = lambda n: jnp.zeros((d, n), w_in.dtype)
    w_kr = w_in[:, O_KR:O_QB]
    w1 = jnp.concatenate([w_in[:, O_CQ:O_KR],
                          z(A_NOPE), w_kr, z(LANES - A_NOPE - A_ROPE),
                          z(A_NOPE), _swap_halves(w_kr), z(LANES - A_NOPE - A_ROPE)], axis=1)
    hq = A_NOPE + A_ROPE
    q1, q2 = [], []
    for h in range(A_HEADS):
        blk = w_uq[:, h * hq:(h + 1) * hq]
        q1.append(_pad_cols(blk, LANES))
        zq = jnp.zeros((w_uq.shape[0], A_NOPE), w_uq.dtype)
        q2.append(_pad_cols(jnp.concatenate([zq, _swap_halves(blk[:, A_NOPE:])], axis=1), LANES))
    wq1 = jnp.concatenate(q1, axis=1)
    wq2 = jnp.concatenate(q2, axis=1)
    wk = jnp.concatenate([_pad_cols(w_uk[:, h * A_NOPE:(h + 1) * A_NOPE], LANES)
                          for h in range(A_HEADS)], axis=1)
    w_ki = w_in[:, O_KI:O_WI]
    w2 = jnp.concatenate([w_in[:, O_QB:O_KI],
                          w_ki, z(LANES - IDX_DIM), z(LANES - IDX_DIM), w_ki,
                          w_in[:, O_WI:O_GATE], z(LANES - IDX_HEADS),
                          w_in[:, O_GATE:]], axis=1)
    bf = lambda a: a.astype(BF16)
    return bf(w1), bf(wq1), bf(wq2), bf(wk), bf(w_uv), bf(w2), bf(w_out)


def _rope_tables(s_len):
    pos = jnp.arange(s_len, dtype=F32)
    freqs = ROPE_THETA ** (-jnp.arange(0, A_ROPE, 2, dtype=F32) / A_ROPE)
    ang = pos[:, None] * freqs[None, :]
    cos, sin = jnp.cos(ang), jnp.sin(ang)
    ones = jnp.ones((s_len, A_NOPE), F32)
    zeros = jnp.zeros((s_len, LANES - A_NOPE - A_ROPE), F32)
    ct = jnp.concatenate([ones, cos, cos, zeros], axis=1)
    st = jnp.concatenate([jnp.zeros((s_len, A_NOPE), F32), -sin, sin, zeros], axis=1)
    return ct, st


def kernel(x, c, w_ada, b_ada, w_in, q_norm_g, kv_norm_g, w_uq, w_uk, w_uv, w_out, ln_g, ln_b):
    bsz, s_len, d = x.shape
    depth = w_ada.shape[0]
    alpha = (2 * depth) ** 0.25
    slopes = 2.0 ** (-8.0 * jnp.arange(1, N_ALIBI + 1, dtype=F32) / N_ALIBI)
    slopes_b, slopes_c = slopes[:B_HEADS], slopes[B_HEADS:]
    ct, st = _rope_tables(s_len)
    mod_all = _mod_call(c, w_ada, b_ada)

    for l in range(depth):
        w1, wq1, wq2, wk, wv, w2, wo = _layer_weights(w_in[l], w_uq[l], w_uk[l], w_uv[l], w_out[l])
        mod = mod_all[l].reshape(bsz, 1, 3 * d)
        qa, ka, va = _proj_mla_call(x, mod, w1, q_norm_g[l].reshape(1, -1), kv_norm_g[l].reshape(1, -1),
                                    wq1, wq2, wk, wv, ct, st)
        qb, kb, vb, qc, kc, vc, qi, ki, wi, gate = _proj_rest_call(x, mod, w2)
        o_a = _mla_call(qa, ka, va)
        obs, lses = [], []
        for window, dil in DILATED_PATTERNS:
            assert window // dil == DIL_N
            o, lse = _dil_call(qb, kb, vb, dil, slopes_b)
            obs.append(o)
            lses.append(lse)
        o_c = _dsa_call(qi, ki, wi, qc, kc, vc, slopes_c)
        x = _out_call(alpha, x, mod, o_a, obs, lses, o_c, gate, wo,
                      ln_g[l].reshape(1, -1), ln_b[l].reshape(1, -1))
    return x
```

```python
import functools

import numpy as np
import jax
import jax.numpy as jnp
from jax import lax
from jax.experimental import pallas as pl
from jax.experimental.pallas import tpu as pltpu

F32 = jnp.float32
BF16 = jnp.bfloat16
I32 = jnp.int32

D_MODEL = 1024
HEAD_DIM = 64
A_HEADS = 6
A_NOPE = 64
A_ROPE = 32
A_V = 64
Q_LORA = 768
KV_LORA = 256
ROPE_THETA = 10000.0
B_HEADS = 6
DILATED_PATTERNS = ((128, 1), (512, 4), (2048, 16))
C_HEADS = 4
IDX_HEADS = 8
IDX_DIM = 64
TOPK_MAX = 256
A_WIDTH = A_HEADS * A_V
B_WIDTH = B_HEADS * HEAD_DIM
C_WIDTH = C_HEADS * HEAD_DIM
MIX_WIDTH = A_WIDTH + B_WIDTH + C_WIDTH
N_ALIBI = B_HEADS + C_HEADS
SPLIT_SIZES = (Q_LORA, KV_LORA, A_ROPE, B_WIDTH, B_WIDTH, B_WIDTH, C_WIDTH, C_WIDTH, C_WIDTH,
               IDX_HEADS * IDX_DIM, IDX_DIM, IDX_HEADS, MIX_WIDTH)
_OFFS = [0] + [int(v) for v in np.cumsum(SPLIT_SIZES)]
(O_CQ, O_CKV, O_KR, O_QB, O_KB, O_VB, O_QC, O_KC, O_VC, O_QI, O_KI, O_WI, O_GATE, _O_END) = _OFFS

LANES = 128
LOG2E = 1.4426950408889634
MASKED = -1e30
VMEM_LIMIT = 48 * 1024 * 1024

PROJ_ROWS = 512
MLA_TQ = 512
DIL_N = 128
DIL_TQ = 512
DSA_Q = 128
DSA_CH = 512
KEY_ALL_FINITE = int(np.int32(np.uint32(0xFF800000) ^ np.uint32(0x7FFFFFFF))) + 1


def _dot(a, b):
    return jnp.dot(a, b, preferred_element_type=F32)


def _dot_nt(a, b):
    return lax.dot_general(a, b, (((1,), (1,)), ((), ())), preferred_element_type=F32)


def _cparams(sem):
    return pltpu.CompilerParams(dimension_semantics=sem, vmem_limit_bytes=VMEM_LIMIT)


def _mod_kernel(c_ref, w_ref, b_ref, o_ref):
    c = c_ref[...]
    sc = (c * jax.nn.sigmoid(c)).astype(BF16)
    o_ref[0] = _dot(sc, w_ref[0].astype(BF16)) + b_ref[0]


def _mod_call(c, w_ada, b_ada):
    depth, d, n3 = w_ada.shape
    bsz = c.shape[0]
    tn = 1024
    return pl.pallas_call(
        _mod_kernel,
        grid=(depth, n3 // tn),
        in_specs=[pl.BlockSpec((bsz, d), lambda l, j: (0, 0)),
                  pl.BlockSpec((1, d, tn), lambda l, j: (l, 0, j)),
                  pl.BlockSpec((1, 1, tn), lambda l, j: (l, 0, j))],
        out_specs=pl.BlockSpec((1, bsz, tn), lambda l, j: (l, 0, j)),
        out_shape=jax.ShapeDtypeStruct((depth, bsz, n3), F32),
        compiler_params=_cparams(("arbitrary", "arbitrary")),
    )(c, w_ada, b_ada.reshape(depth, 1, n3))


def _modulate(x_ref, mod_ref):
    mod = mod_ref[...]
    shift = mod[:, :D_MODEL]
    scale = mod[:, D_MODEL:2 * D_MODEL]
    return (x_ref[...] * (1.0 + scale) + shift).astype(BF16)


def _rms(x, g, eps=1e-6):
    ms = jnp.mean(x * x, axis=-1, keepdims=True)
    return (x * lax.rsqrt(ms + eps) * g).astype(BF16)


def _proj_mla_kernel(x_ref, mod_ref, w1_ref, gq_ref, gkv_ref, wq1_ref, wq2_ref, wk_ref, wv_ref,
                     ct_ref, st_ref, q_ref, k_ref, v_ref):
    h = _modulate(x_ref, mod_ref)
    pa = _dot(h, w1_ref[...])
    cq = _rms(pa[:, :Q_LORA], gq_ref[...])
    ckv = _rms(pa[:, Q_LORA:Q_LORA + KV_LORA], gkv_ref[...])
    ct = ct_ref[...]
    st = st_ref[...]
    ct6 = jnp.concatenate([ct] * A_HEADS, axis=1)
    st6 = jnp.concatenate([st] * A_HEADS, axis=1)
    qscale = (A_NOPE + A_ROPE) ** -0.5 * LOG2E
    q = (_dot(cq, wq1_ref[...]) * ct6 + _dot(cq, wq2_ref[...]) * st6) * qscale
    q_ref[...] = q.astype(BF16)
    o = Q_LORA + KV_LORA
    kr = pa[:, o:o + LANES] * ct + pa[:, o + LANES:o + 2 * LANES] * st
    k = _dot(ckv, wk_ref[...]) + jnp.concatenate([kr] * A_HEADS, axis=1)
    k_ref[...] = k.astype(BF16)
    v_ref[...] = _dot(ckv, wv_ref[...]).astype(BF16)


def _proj_mla_call(x, mod, w1, gq, gkv, wq1, wq2, wk, wv, ct, st):
    bsz, s_len, d = x.shape
    tm = PROJ_ROWS
    hw = A_HEADS * LANES
    full = lambda a: pl.BlockSpec(a.shape, lambda b, i: (0,) * a.ndim)
    return pl.pallas_call(
        _proj_mla_kernel,
        grid=(bsz, s_len // tm),
        in_specs=[pl.BlockSpec((None, tm, d), lambda b, i: (b, i, 0)),
                  pl.BlockSpec((None, 1, 3 * d), lambda b, i: (b, 0, 0)),
                  full(w1), full(gq), full(gkv), full(wq1), full(wq2), full(wk), full(wv),
                  pl.BlockSpec((tm, LANES), lambda b, i: (i, 0)),
                  pl.BlockSpec((tm, LANES), lambda b, i: (i, 0))],
        out_specs=[pl.BlockSpec((None, tm, hw), lambda b, i: (b, i, 0)),
                   pl.BlockSpec((None, tm, hw), lambda b, i: (b, i, 0)),
                   pl.BlockSpec((None, tm, A_WIDTH), lambda b, i: (b, i, 0))],
        out_shape=[jax.ShapeDtypeStruct((bsz, s_len, hw), BF16),
                   jax.ShapeDtypeStruct((bsz, s_len, hw), BF16),
                   jax.ShapeDtypeStruct((bsz, s_len, A_WIDTH), BF16)],
        compiler_params=_cparams(("parallel", "parallel")),
    )(x, mod, w1, gq, gkv, wq1, wq2, wk, wv, ct, st)


_R_QB, _R_KB, _R_VB = 0, B_WIDTH, 2 * B_WIDTH
_R_QC = 3 * B_WIDTH
_R_KC, _R_VC = _R_QC + C_WIDTH, _R_QC + 2 * C_WIDTH
_R_QI = _R_QC + 3 * C_WIDTH
_R_KI = _R_QI + IDX_HEADS * IDX_DIM
_R_WI = _R_KI + 2 * LANES
_R_GATE = _R_WI + LANES
_R_END = _R_GATE + MIX_WIDTH


def _proj_rest_kernel(x_ref, mod_ref, w2_ref, qb_ref, kb_ref, vb_ref, qc_ref, kc_ref, vc_ref,
                      qi_ref, ki_ref, wi_ref, gate_ref):
    h = _modulate(x_ref, mod_ref)

    def proj(start, width):
        return _dot(h, w2_ref[:, start:start + width])

    inv_sqrt_dh = HEAD_DIM ** -0.5 * LOG2E
    qb_ref[...] = (proj(_R_QB, B_WIDTH) * inv_sqrt_dh).astype(BF16)
    kb_ref[...] = proj(_R_KB, B_WIDTH).astype(BF16)
    vb_ref[...] = proj(_R_VB, B_WIDTH).astype(BF16)
    qc_ref[...] = (proj(_R_QC, C_WIDTH) * inv_sqrt_dh).astype(BF16)
    kc_ref[...] = proj(_R_KC, C_WIDTH).astype(BF16)
    vc_ref[...] = proj(_R_VC, C_WIDTH).astype(BF16)
    qi_ref[...] = (proj(_R_QI, IDX_HEADS * IDX_DIM) * IDX_DIM ** -0.5).astype(BF16)
    ki_ref[...] = proj(_R_KI, 2 * LANES).astype(BF16)
    wi_ref[...] = proj(_R_WI, LANES) * IDX_HEADS ** -0.5
    gate_ref[...] = proj(_R_GATE, MIX_WIDTH)


def _proj_rest_call(x, mod, w2):
    bsz, s_len, d = x.shape
    tm = PROJ_ROWS
    widths = (B_WIDTH, B_WIDTH, B_WIDTH, C_WIDTH, C_WIDTH, C_WIDTH, IDX_HEADS * IDX_DIM,
              2 * LANES, LANES, MIX_WIDTH)
    dtypes = (BF16,) * 8 + (F32, F32)
    return pl.pallas_call(
        _proj_rest_kernel,
        grid=(bsz, s_len // tm),
        in_specs=[pl.BlockSpec((None, tm, d), lambda b, i: (b, i, 0)),
                  pl.BlockSpec((None, 1, 3 * d), lambda b, i: (b, 0, 0)),
                  pl.BlockSpec(w2.shape, lambda b, i: (0, 0))],
        out_specs=[pl.BlockSpec((None, tm, w), lambda b, i: (b, i, 0)) for w in widths],
        out_shape=[jax.ShapeDtypeStruct((bsz, s_len, w), dt) for w, dt in zip(widths, dtypes)],
        compiler_params=_cparams(("parallel", "parallel")),
    )(x, mod, w2)


def _online_update(s, v_ext, m_ref, acc_ref, idx):
    m_prev = m_ref[idx]
    m_new = jnp.maximum(m_prev, jnp.max(s, axis=1, keepdims=True))
    alpha = jnp.exp2(m_prev - m_new)
    p = jnp.exp2(s - jnp.tile(m_new, (1, s.shape[1] // LANES)))
    acc_ref[idx] = jnp.tile(alpha, (1, 2)) * acc_ref[idx] + _dot(p.astype(BF16), v_ext)
    m_ref[idx] = m_new


def _with_ones(v):
    return jnp.concatenate([v, jnp.ones_like(v)], axis=1)


def _init_state(m_ref, acc_ref):
    m_ref[...] = jnp.full(m_ref.shape, MASKED, F32)
    acc_ref[...] = jnp.zeros(acc_ref.shape, F32)


def _pair_output(acc_ref, i0, i1):
    lane = lax.broadcasted_iota(I32, (acc_ref.shape[1], LANES), 1)
    a0, a1 = acc_ref[i0], acc_ref[i1]
    o0 = a0[:, :LANES] / a0[:, LANES:]
    o1 = a1[:, :LANES] / a1[:, LANES:]
    return jnp.where(lane < HEAD_DIM, o0, o1)


def _mla_kernel(q_ref, k_ref, v_ref, o_ref, m_ref, acc_ref):
    qi = pl.program_id(2)
    tq = q_ref.shape[0]
    _init_state(m_ref, acc_ref)

    def scores(hh, c):
        q = q_ref[:, hh * LANES:(hh + 1) * LANES]
        k = k_ref[pl.ds(pl.multiple_of(c * tq, tq), tq), hh * LANES:(hh + 1) * LANES]
        return _dot_nt(q, k)

    def body(c, carry):
        v = _with_ones(v_ref[pl.ds(pl.multiple_of(c * tq, tq), tq), :])
        for hh in range(2):
            _online_update(scores(hh, c), v, m_ref, acc_ref, hh)
        return carry

    lax.fori_loop(0, qi, body, 0)
    row = lax.broadcasted_iota(I32, (tq, tq), 0)
    col = lax.broadcasted_iota(I32, (tq, tq), 1)
    v = _with_ones(v_ref[pl.ds(pl.multiple_of(qi * tq, tq), tq), :])
    for hh in range(2):
        s = jnp.where(col <= row, scores(hh, qi), MASKED)
        _online_update(s, v, m_ref, acc_ref, hh)
    o_ref[...] = _pair_output(acc_ref, 0, 1)


def _mla_call(q, k, v):
    bsz, s_len, _ = q.shape
    tq = MLA_TQ
    npair = A_HEADS // 2
    return pl.pallas_call(
        _mla_kernel,
        grid=(bsz, npair, s_len // tq),
        in_specs=[pl.BlockSpec((None, tq, 2 * LANES), lambda b, p, i: (b, i, p)),
                  pl.BlockSpec((None, s_len, 2 * LANES), lambda b, p, i: (b, 0, p)),
                  pl.BlockSpec((None, s_len, LANES), lambda b, p, i: (b, 0, p))],
        out_specs=pl.BlockSpec((None, tq, LANES), lambda b, p, i: (b, i, p)),
        out_shape=jax.ShapeDtypeStruct((bsz, s_len, A_WIDTH), F32),
        scratch_shapes=[pltpu.VMEM((2, tq, LANES), F32), pltpu.VMEM((2, tq, 2 * LANES), F32)],
        compiler_params=_cparams(("parallel", "parallel", "arbitrary")),
    )(q, k, v)


def _dil_kernel(q_ref, k_ref, v_ref, bias_ref, o_ref, lse_ref):
    i = pl.program_id(2)
    n = DIL_N
    blocks = q_ref.shape[0] // n
    lane = lax.broadcasted_iota(I32, (2 * n, LANES), 1)
    first = lax.broadcasted_iota(I32, (n, LANES), 1) < HEAD_DIM
    for blk in range(blocks):
        j = i * blocks + blk
        start = pl.multiple_of(jnp.maximum(j - 1, 0) * n, n)
        table = jnp.minimum(j, 1)
        for p in range(B_HEADS // 2):
            cols = slice(p * LANES, (p + 1) * LANES)
            q = q_ref[blk * n:(blk + 1) * n, cols]
            k = k_ref[pl.ds(start, 2 * n), cols]
            v = _with_ones(v_ref[pl.ds(start, 2 * n), cols])
            outs, lses = [], []
            for hh in range(2):
                h = 2 * p + hh
                own = (lane >= hh * HEAD_DIM) & (lane < (hh + 1) * HEAD_DIM)
                s = _dot_nt(q, jnp.where(own, k, jnp.zeros_like(k))) + bias_ref[table, h]
                m = jnp.max(s, axis=1, keepdims=True)
                oe = _dot(jnp.exp2(s - m).astype(BF16), v)
                den = oe[:, LANES:]
                outs.append(oe[:, :LANES] / den)
                lses.append(m + jnp.log2(den))
            o_ref[blk * n:(blk + 1) * n, cols] = jnp.where(first, outs[0], outs[1])
            lse_ref[blk * n:(blk + 1) * n, cols] = jnp.where(first, lses[0], lses[1])


def _dil_bias(dil, slopes_b):
    n = DIL_N
    kj = jnp.arange(2 * n)
    step = jnp.arange(n)[:, None] + n - kj[None, :]
    valid = (step >= 0) & (step <= n)
    bias = -(slopes_b * LOG2E)[:, None, None] * (step * dil).astype(F32)[None]
    banded = jnp.where(valid[None], bias, MASKED)
    first = jnp.concatenate([banded[:, :, n:], jnp.full_like(banded[:, :, n:], MASKED)], axis=2)
    return jnp.stack([first, banded])


def _dil_call(q, k, v, dil, slopes_b):
    bsz, s_len, w = q.shape
    L = s_len // dil
    tq = min(DIL_TQ, L)
    view = lambda a: a.reshape(bsz, L, dil * w)
    bias = _dil_bias(dil, slopes_b)
    o, lse = pl.pallas_call(
        _dil_kernel,
        grid=(bsz, dil, L // tq),
        in_specs=[pl.BlockSpec((None, tq, w), lambda b, r, i: (b, i, r)),
                  pl.BlockSpec((None, L, w), lambda b, r, i: (b, 0, r)),
                  pl.BlockSpec((None, L, w), lambda b, r, i: (b, 0, r)),
                  pl.BlockSpec(bias.shape, lambda b, r, i: (0, 0, 0, 0))],
        out_specs=[pl.BlockSpec((None, tq, w), lambda b, r, i: (b, i, r))] * 2,
        out_shape=[jax.ShapeDtypeStruct((bsz, L, dil * w), F32)] * 2,
        compiler_params=_cparams(("parallel", "parallel", "arbitrary")),
    )(view(q), view(k), view(v), bias)
    return o.reshape(bsz, s_len, w), lse.reshape(bsz, s_len, w)


def _sortable(x):
    b = lax.bitcast_convert_type(x, I32)
    return b ^ ((b >> 31) & 0x7FFFFFFF)


def _dsa_kernel(qi_ref, ki_ref, wi_ref, qc_ref, kc_ref, vc_ref, tri_ref, slope_ref, o_ref,
                key_ref, sel_ref, thr_ref, m_ref, acc_ref):
    qb = pl.program_id(1)
    nq, ch = DSA_Q, DSA_CH
    q0 = qb * nq
    nk = (q0 + nq + ch - 1) // ch
    row = lax.broadcasted_iota(I32, (nq, ch), 0)
    col = lax.broadcasted_iota(I32, (nq, ch), 1)
    rel = col - row

    w = wi_ref[...]

    def score_body(c, carry):
        ks = pl.ds(pl.multiple_of(c * ch, ch), ch)
        k_lo = ki_ref[ks, :LANES]
        k_hi = ki_ref[ks, LANES:]
        sc = jnp.zeros((nq, ch), F32)
        for j in range(IDX_HEADS // 2):
            qj = qi_ref[:, j * LANES:(j + 1) * LANES]
            sc = sc + w[:, 2 * j:2 * j + 1] * jnp.maximum(_dot_nt(qj, k_lo), 0.0)
            sc = sc + w[:, 2 * j + 1:2 * j + 2] * jnp.maximum(_dot_nt(qj, k_hi), 0.0)
        causal = rel <= q0 - c * ch
        key_ref[c] = _sortable(jnp.where(causal, sc, -jnp.inf))
        return carry

    lax.fori_loop(0, nk, score_body, 0)

    def count(pred):
        def body(c, acc):
            hit = jnp.where(pred(key_ref[c]), 1.0, 0.0)
            part = hit[:, :LANES]
            for t in range(1, ch // LANES):
                part = part + hit[:, t * LANES:(t + 1) * LANES]
            return acc + part
        acc = lax.fori_loop(0, nk, body, jnp.zeros((nq, LANES), F32))
        return jnp.sum(acc, axis=1, keepdims=True)

    topk = float(TOPK_MAX)
    thr_ref[...] = jnp.full(thr_ref.shape, KEY_ALL_FINITE, I32)
    need_tie = jnp.zeros((), jnp.bool_)

    @pl.when(q0 + 1 > TOPK_MAX)
    def _():
        zero = jnp.zeros((nq, 1), I32)
        r0 = jnp.where(count(lambda kk: kk >= zero) >= topk, zero, jnp.full((nq, 1), -2 ** 31, I32))

        def bit_body(it, r):
            cand = r + jnp.left_shift(jnp.int32(1), 30 - it)
            return jnp.where(count(lambda kk: kk >= cand) >= topk, cand, r)

        thr_ref[...] = jnp.broadcast_to(lax.fori_loop(0, 31, bit_body, r0), thr_ref.shape)

    thr = thr_ref[:, :1]
    n_ge = count(lambda kk: kk >= thr)
    has_tie = jnp.logical_and(q0 + 1 > TOPK_MAX, jnp.max(n_ge) > topk)

    @pl.when(jnp.logical_not(has_tie))
    def _():
        def body(c, carry):
            sel_ref[c] = jnp.where(key_ref[c] >= thr, 0.0, MASKED)
            return carry
        lax.fori_loop(0, nk, body, 0)

    @pl.when(has_tie)
    def _():
        need = topk - count(lambda kk: kk > thr)

        def body(c, seen):
            kk = key_ref[c]
            eq = kk == thr
            rank = seen + _dot(jnp.where(eq, 1.0, 0.0).astype(BF16), tri_ref[...])
            take = (kk > thr) | (eq & (rank <= need))
            sel_ref[c] = jnp.where(take, 0.0, MASKED)
            return rank[:, ch - 1:ch]
        lax.fori_loop(0, nk, body, jnp.zeros((nq, 1), F32))

    _init_state(m_ref, acc_ref)
    lane = lax.broadcasted_iota(I32, (ch, LANES), 1)
    relf = rel.astype(F32)

    def attn_body(c, carry):
        ks = pl.ds(pl.multiple_of(c * ch, ch), ch)
        dist = relf + (c * ch - q0).astype(F32)
        sel = sel_ref[c]
        for p in range(C_HEADS // 2):
            cols = slice(p * LANES, (p + 1) * LANES)
            q = qc_ref[:, cols]
            k = kc_ref[ks, cols]
            v = _with_ones(vc_ref[ks, cols])
            for hh in range(2):
                h = 2 * p + hh
                own = (lane >= hh * HEAD_DIM) & (lane < (hh + 1) * HEAD_DIM)
                s = _dot_nt(q, jnp.where(own, k, jnp.zeros_like(k)))
                _online_update(s + (slope_ref[h] * dist + sel), v, m_ref, acc_ref, h)
        return carry

    lax.fori_loop(0, nk, attn_body, 0)
    for p in range(C_HEADS // 2):
        o_ref[:, p * LANES:(p + 1) * LANES] = _pair_output(acc_ref, 2 * p, 2 * p + 1)


def _dsa_call(qi, ki, wi, qc, kc, vc, slopes_c):
    bsz, s_len, _ = qc.shape
    nq, ch = DSA_Q, DSA_CH
    tri = (jnp.arange(ch)[:, None] <= jnp.arange(ch)[None, :]).astype(BF16)
    blk = lambda w: pl.BlockSpec((None, nq, w), lambda b, i: (b, i, 0))
    seq = lambda w: pl.BlockSpec((None, s_len, w), lambda b, i: (b, 0, 0))
    return pl.pallas_call(
        _dsa_kernel,
        grid=(bsz, s_len // nq),
        in_specs=[blk(IDX_HEADS * IDX_DIM), seq(2 * LANES), blk(LANES),
                  blk(C_WIDTH), seq(C_WIDTH), seq(C_WIDTH),
                  pl.BlockSpec((ch, ch), lambda b, i: (0, 0)),
                  pl.BlockSpec(memory_space=pltpu.SMEM)],
        out_specs=blk(C_WIDTH),
        out_shape=jax.ShapeDtypeStruct((bsz, s_len, C_WIDTH), F32),
        scratch_shapes=[pltpu.VMEM((s_len // ch, nq, ch), I32),
                        pltpu.VMEM((s_len // ch, nq, ch), F32),
                        pltpu.VMEM((nq, LANES), I32),
                        pltpu.VMEM((C_HEADS, nq, LANES), F32),
                        pltpu.VMEM((C_HEADS, nq, 2 * LANES), F32)],
        compiler_params=_cparams(("parallel", "arbitrary")),
    )(qi, ki, wi, qc, kc, vc, tri, slopes_c * LOG2E)


def _out_kernel(alpha, x_ref, mod_ref, oa_ref, ob1_ref, ob2_ref, ob3_ref, l1_ref, l2_ref, l3_ref,
                oc_ref, gate_ref, wo_ref, g_ref, b_ref, y_ref):
    l1, l2, l3 = l1_ref[...], l2_ref[...], l3_ref[...]
    mx = jnp.maximum(jnp.maximum(l1, l2), l3)
    e1, e2, e3 = jnp.exp2(l1 - mx), jnp.exp2(l2 - mx), jnp.exp2(l3 - mx)
    ob = (e1 * ob1_ref[...] + e2 * ob2_ref[...] + e3 * ob3_ref[...]) / (e1 + e2 + e3)
    pg = gate_ref[...]
    sg = pg * jax.nn.sigmoid(pg)
    ya = (oa_ref[...] * sg[:, :A_WIDTH]).astype(BF16)
    yb = (ob * sg[:, A_WIDTH:A_WIDTH + B_WIDTH]).astype(BF16)
    yc = (oc_ref[...] * sg[:, A_WIDTH + B_WIDTH:]).astype(BF16)
    sub = (_dot(ya, wo_ref[:A_WIDTH, :]) + _dot(yb, wo_ref[A_WIDTH:A_WIDTH + B_WIDTH, :])
           + _dot(yc, wo_ref[A_WIDTH + B_WIDTH:, :]))
    gate = mod_ref[...][:, 2 * D_MODEL:]
    z = alpha * x_ref[...] + (1.0 + gate) * sub
    mu = jnp.mean(z, axis=-1, keepdims=True)
    zc = z - mu
    var = jnp.mean(zc * zc, axis=-1, keepdims=True)
    y_ref[...] = zc * lax.rsqrt(var + 1e-5) * g_ref[...] + b_ref[...]


def _out_call(alpha, x, mod, oa, obs, lses, oc, gate, wo, g, b):
    bsz, s_len, d = x.shape
    tm = PROJ_ROWS
    rows = lambda w: pl.BlockSpec((None, tm, w), lambda bb, i: (bb, i, 0))
    full = lambda a: pl.BlockSpec(a.shape, lambda bb, i: (0,) * a.ndim)
    return pl.pallas_call(
        functools.partial(_out_kernel, alpha),
        grid=(bsz, s_len // tm),
        in_specs=[rows(d), pl.BlockSpec((None, 1, 3 * d), lambda bb, i: (bb, 0, 0)),
                  rows(A_WIDTH)] + [rows(B_WIDTH)] * 6 + [rows(C_WIDTH), rows(MIX_WIDTH),
                  full(wo), full(g), full(b)],
        out_specs=rows(d),
        out_shape=jax.ShapeDtypeStruct((bsz, s_len, d), F32),
        compiler_params=_cparams(("parallel", "parallel")),
    )(x, mod, oa, *obs, *lses, oc, gate, wo, g, b)


def _pad_cols(a, width):
    return jnp.pad(a, ((0, 0), (0, width - a.shape[1])))


def _swap_halves(a):
    half = a.shape[1] // 2
    return jnp.concatenate([a[:, half:], a[:, :half]], axis=1)


def _layer_weights(w_in, w_uq, w_uk, w_uv, w_out):
    d = w_in.shape[0]
    z = lambda n: jnp.zeros((d, n), w_in.dtype)
    w_kr = w_in[:, O_KR:O_QB]
    w1 = jnp.concatenate([w_in[:, O_CQ:O_KR],
                          z(A_NOPE), w_kr, z(LANES - A_NOPE - A_ROPE),
                          z(A_NOPE), _swap_halves(w_kr), z(LANES - A_NOPE - A_ROPE)], axis=1)
    hq = A_NOPE + A_ROPE
    q1, q2 = [], []
    for h in range(A_HEADS):
        blk = w_uq[:, h * hq:(h + 1) * hq]
        q1.append(_pad_cols(blk, LANES))
        zq = jnp.zeros((w_uq.shape[0], A_NOPE), w_uq.dtype)
        q2.append(_pad_cols(jnp.concatenate([zq, _swap_halves(blk[:, A_NOPE:])], axis=1), LANES))
    wq1 = jnp.concatenate(q1, axis=1)
    wq2 = jnp.concatenate(q2, axis=1)
    wk = jnp.concatenate([_pad_cols(w_uk[:, h * A_NOPE:(h + 1) * A_NOPE], LANES)
                          for h in range(A_HEADS)], axis=1)
    w_ki = w_in[:, O_KI:O_WI]
    w2 = jnp.concatenate([w_in[:, O_QB:O_KI],
                          w_ki, z(LANES - IDX_DIM), z(LANES - IDX_DIM), w_ki,
                          w_in[:, O_WI:O_GATE], z(LANES - IDX_HEADS),
                          w_in[:, O_GATE:]], axis=1)
    bf = lambda a: a.astype(BF16)
    return bf(w1), bf(wq1), bf(wq2), bf(wk), bf(w_uv), bf(w2), bf(w_out)


def _rope_tables(s_len):
    pos = jnp.arange(s_len, dtype=F32)
    freqs = ROPE_THETA ** (-jnp.arange(0, A_ROPE, 2, dtype=F32) / A_ROPE)
    ang = pos[:, None] * freqs[None, :]
    cos, sin = jnp.cos(ang), jnp.sin(ang)
    ones = jnp.ones((s_len, A_NOPE), F32)
    zeros = jnp.zeros((s_len, LANES - A_NOPE - A_ROPE), F32)
    ct = jnp.concatenate([ones, cos, cos, zeros], axis=1)
    st = jnp.concatenate([jnp.zeros((s_len, A_NOPE), F32), -sin, sin, zeros], axis=1)
    return ct, st


def kernel(x, c, w_ada, b_ada, w_in, q_norm_g, kv_norm_g, w_uq, w_uk, w_uv, w_out, ln_g, ln_b):
    bsz, s_len, d = x.shape
    depth = w_ada.shape[0]
    alpha = (2 * depth) ** 0.25
    slopes = 2.0 ** (-8.0 * jnp.arange(1, N_ALIBI + 1, dtype=F32) / N_ALIBI)
    slopes_b, slopes_c = slopes[:B_HEADS], slopes[B_HEADS:]
    ct, st = _rope_tables(s_len)
    mod_all = _mod_call(c, w_ada, b_ada)

    for l in range(depth):
        w1, wq1, wq2, wk, wv, w2, wo = _layer_weights(w_in[l], w_uq[l], w_uk[l], w_uv[l], w_out[l])
        mod = mod_all[l].reshape(bsz, 1, 3 * d)
        qa, ka, va = _proj_mla_call(x, mod, w1, q_norm_g[l].reshape(1, -1), kv_norm_g[l].reshape(1, -1),
                                    wq1, wq2, wk, wv, ct, st)
        qb, kb, vb, qc, kc, vc, qi, ki, wi, gate = _proj_rest_call(x, mod, w2)
        o_a = _mla_call(qa, ka, va)
        obs, lses = [], []
        for window, dil in DILATED_PATTERNS:
            assert window // dil == DIL_N
            o, lse = _dil_call(qb, kb, vb, dil, slopes_b)
            obs.append(o)
            lses.append(lse)
        o_c = _dsa_call(qi, ki, wi, qc, kc, vc, slopes_c)
        x = _out_call(alpha, x, mod, o_a, obs, lses, o_c, gate, wo,
                      ln_g[l].reshape(1, -1), ln_b[l].reshape(1, -1))
    return x
```

```python
import functools

import numpy as np
import jax
import jax.numpy as jnp
from jax import lax
from jax.experimental import pallas as pl
from jax.experimental.pallas import tpu as pltpu

F32 = jnp.float32
BF16 = jnp.bfloat16
I32 = jnp.int32

D_MODEL = 1024
HEAD_DIM = 64
A_HEADS = 6
A_NOPE = 64
A_ROPE = 32
A_V = 64
Q_LORA = 768
KV_LORA = 256
ROPE_THETA = 10000.0
B_HEADS = 6
DILATED_PATTERNS = ((128, 1), (512, 4), (2048, 16))
C_HEADS = 4
IDX_HEADS = 8
IDX_DIM = 64
TOPK_MAX = 256
A_WIDTH = A_HEADS * A_V
B_WIDTH = B_HEADS * HEAD_DIM
C_WIDTH = C_HEADS * HEAD_DIM
MIX_WIDTH = A_WIDTH + B_WIDTH + C_WIDTH
N_ALIBI = B_HEADS + C_HEADS
SPLIT_SIZES = (Q_LORA, KV_LORA, A_ROPE, B_WIDTH, B_WIDTH, B_WIDTH, C_WIDTH, C_WIDTH, C_WIDTH,
               IDX_HEADS * IDX_DIM, IDX_DIM, IDX_HEADS, MIX_WIDTH)
_OFFS = [0] + [int(v) for v in np.cumsum(SPLIT_SIZES)]
(O_CQ, O_CKV, O_KR, O_QB, O_KB, O_VB, O_QC, O_KC, O_VC, O_QI, O_KI, O_WI, O_GATE, _O_END) = _OFFS

LANES = 128
LOG2E = 1.4426950408889634
MASKED = -1e30
VMEM_LIMIT = 48 * 1024 * 1024

PROJ_ROWS = 512
MLA_SPLIT = 4
MLA_TQ = 512
DIL_N = 128
DIL_TQ = 512
DSA_Q = 128
DSA_CH = 512
FLT_MAX = float(np.finfo(np.float32).max)
_KEY_NEG_MAX = int(np.int32(np.uint32(0xFF7FFFFF) ^ np.uint32(0x7FFFFFFF)))
_KEY_FIRST_NAN = 0x7F800001


def _dot(a, b):
    return jnp.dot(a, b, preferred_element_type=F32)


def _dot_nt(a, b):
    return lax.dot_general(a, b, (((1,), (1,)), ((), ())), preferred_element_type=F32)


def _cparams(sem):
    return pltpu.CompilerParams(dimension_semantics=sem, vmem_limit_bytes=VMEM_LIMIT)


def _mod_kernel(c_ref, w_ref, b_ref, o_ref):
    c = c_ref[...]
    sc = (c * jax.nn.sigmoid(c)).astype(BF16)
    o_ref[0] = _dot(sc, w_ref[0].astype(BF16)) + b_ref[0]


def _mod_call(c, w_ada, b_ada):
    depth, d, n3 = w_ada.shape
    bsz = c.shape[0]
    tn = 1024
    return pl.pallas_call(
        _mod_kernel,
        grid=(depth, n3 // tn),
        in_specs=[pl.BlockSpec((bsz, d), lambda l, j: (0, 0)),
                  pl.BlockSpec((1, d, tn), lambda l, j: (l, 0, j)),
                  pl.BlockSpec((1, 1, tn), lambda l, j: (l, 0, j))],
        out_specs=pl.BlockSpec((1, bsz, tn), lambda l, j: (l, 0, j)),
        out_shape=jax.ShapeDtypeStruct((depth, bsz, n3), F32),
        compiler_params=_cparams(("arbitrary", "arbitrary")),
    )(c, w_ada, b_ada.reshape(depth, 1, n3))


def _modulate(x_ref, mod_ref):
    mod = mod_ref[...]
    shift = mod[:, :D_MODEL]
    scale = mod[:, D_MODEL:2 * D_MODEL]
    return (x_ref[...] * (1.0 + scale) + shift).astype(BF16)


def _rms(x, g, eps=1e-6):
    ms = jnp.mean(x * x, axis=-1, keepdims=True)
    return (x * lax.rsqrt(ms + eps) * g).astype(BF16)


def _proj_mla_kernel(x_ref, mod_ref, w1_ref, gq_ref, gkv_ref, wq1_ref, wq2_ref, wk_ref, wv_ref,
                     ct_ref, st_ref, q_ref, k_ref, v_ref):
    h = _modulate(x_ref, mod_ref)
    pa = _dot(h, w1_ref[...])
    cq = _rms(pa[:, :Q_LORA], gq_ref[...])
    ckv = _rms(pa[:, Q_LORA:Q_LORA + KV_LORA], gkv_ref[...])
    ct = ct_ref[...]
    st = st_ref[...]
    ct6 = jnp.concatenate([ct] * A_HEADS, axis=1)
    st6 = jnp.concatenate([st] * A_HEADS, axis=1)
    qscale = (A_NOPE + A_ROPE) ** -0.5 * LOG2E
    q = (_dot(cq, wq1_ref[...]) * ct6 + _dot(cq, wq2_ref[...]) * st6) * qscale
    q_ref[...] = q.astype(BF16)
    o = Q_LORA + KV_LORA
    kr = pa[:, o:o + LANES] * ct + pa[:, o + LANES:o + 2 * LANES] * st
    k = _dot(ckv, wk_ref[...]) + jnp.concatenate([kr] * A_HEADS, axis=1)
    k_ref[...] = k.astype(BF16)
    v_ref[...] = _dot(ckv, wv_ref[...]).astype(BF16)


def _proj_mla_call(x, mod, w1, gq, gkv, wq1, wq2, wk, wv, ct, st):
    bsz, s_len, d = x.shape
    tm = PROJ_ROWS
    hw = A_HEADS * LANES
    full = lambda a: pl.BlockSpec(a.shape, lambda b, i: (0,) * a.ndim)
    return pl.pallas_call(
        _proj_mla_kernel,
        grid=(bsz, s_len // tm),
        in_specs=[pl.BlockSpec((None, tm, d), lambda b, i: (b, i, 0)),
                  pl.BlockSpec((None, 1, 3 * d), lambda b, i: (b, 0, 0)),
                  full(w1), full(gq), full(gkv), full(wq1), full(wq2), full(wk), full(wv),
                  pl.BlockSpec((tm, LANES), lambda b, i: (i, 0)),
                  pl.BlockSpec((tm, LANES), lambda b, i: (i, 0))],
        out_specs=[pl.BlockSpec((None, tm, hw), lambda b, i: (b, i, 0)),
                   pl.BlockSpec((None, tm, hw), lambda b, i: (b, i, 0)),
                   pl.BlockSpec((None, tm, A_WIDTH), lambda b, i: (b, i, 0))],
        out_shape=[jax.ShapeDtypeStruct((bsz, s_len, hw), BF16),
                   jax.ShapeDtypeStruct((bsz, s_len, hw), BF16),
                   jax.ShapeDtypeStruct((bsz, s_len, A_WIDTH), BF16)],
        compiler_params=_cparams(("parallel", "parallel")),
    )(x, mod, w1, gq, gkv, wq1, wq2, wk, wv, ct, st)


_R_QB, _R_KB, _R_VB = 0, B_WIDTH, 2 * B_WIDTH
_R_QC = 3 * B_WIDTH
_R_KC, _R_VC = _R_QC + C_WIDTH, _R_QC + 2 * C_WIDTH
_R_QI = _R_QC + 3 * C_WIDTH
_R_KI = _R_QI + IDX_HEADS * IDX_DIM
_R_WI = _R_KI + 2 * LANES
_R_GATE = _R_WI + LANES
_R_END = _R_GATE + MIX_WIDTH


def _proj_rest_kernel(x_ref, mod_ref, w2_ref, qb_ref, kb_ref, vb_ref, qc_ref, kc_ref, vc_ref,
                      qi_ref, ki_ref, wi_ref, gate_ref):
    h = _modulate(x_ref, mod_ref)

    def proj(start, width):
        return _dot(h, w2_ref[:, start:start + width])

    inv_sqrt_dh = HEAD_DIM ** -0.5 * LOG2E
    qb_ref[...] = (proj(_R_QB, B_WIDTH) * inv_sqrt_dh).astype(BF16)
    kb_ref[...] = proj(_R_KB, B_WIDTH).astype(BF16)
    vb_ref[...] = proj(_R_VB, B_WIDTH).astype(BF16)
    qc_ref[...] = (proj(_R_QC, C_WIDTH) * inv_sqrt_dh).astype(BF16)
    kc_ref[...] = proj(_R_KC, C_WIDTH).astype(BF16)
    vc_ref[...] = proj(_R_VC, C_WIDTH).astype(BF16)
    qi_ref[...] = (proj(_R_QI, IDX_HEADS * IDX_DIM) * IDX_DIM ** -0.5).astype(BF16)
    ki_ref[...] = proj(_R_KI, 2 * LANES).astype(BF16)
    wi_ref[...] = proj(_R_WI, LANES) * IDX_HEADS ** -0.5
    gate_ref[...] = proj(_R_GATE, MIX_WIDTH)


def _proj_rest_call(x, mod, w2):
    bsz, s_len, d = x.shape
    tm = PROJ_ROWS
    widths = (B_WIDTH, B_WIDTH, B_WIDTH, C_WIDTH, C_WIDTH, C_WIDTH, IDX_HEADS * IDX_DIM,
              2 * LANES, LANES, MIX_WIDTH)
    dtypes = (BF16,) * 8 + (F32, F32)
    return pl.pallas_call(
        _proj_rest_kernel,
        grid=(bsz, s_len // tm),
        in_specs=[pl.BlockSpec((None, tm, d), lambda b, i: (b, i, 0)),
                  pl.BlockSpec((None, 1, 3 * d), lambda b, i: (b, 0, 0)),
                  pl.BlockSpec(w2.shape, lambda b, i: (0, 0))],
        out_specs=[pl.BlockSpec((None, tm, w), lambda b, i: (b, i, 0)) for w in widths],
        out_shape=[jax.ShapeDtypeStruct((bsz, s_len, w), dt) for w, dt in zip(widths, dtypes)],
        compiler_params=_cparams(("parallel", "parallel")),
    )(x, mod, w2)


def _online_update(s, v_ext, m_ref, acc_ref, idx):
    m_prev = m_ref[idx]
    m_new = jnp.maximum(m_prev, jnp.max(s, axis=1, keepdims=True))
    alpha = jnp.exp2(m_prev - m_new)
    p = jnp.exp2(s - jnp.tile(m_new, (1, s.shape[1] // LANES)))
    acc_ref[idx] = jnp.tile(alpha, (1, 2)) * acc_ref[idx] + _dot(p.astype(BF16), v_ext)
    m_ref[idx] = m_new


def _with_ones(v):
    return jnp.concatenate([v, jnp.ones_like(v)], axis=1)


def _init_state(m_ref, acc_ref):
    m_ref[...] = jnp.full(m_ref.shape, MASKED, F32)
    acc_ref[...] = jnp.zeros(acc_ref.shape, F32)


def _pair_output(acc_ref, i0, i1):
    lane = lax.broadcasted_iota(I32, (acc_ref.shape[1], LANES), 1)
    a0, a1 = acc_ref[i0], acc_ref[i1]
    o0 = a0[:, :LANES] / a0[:, LANES:]
    o1 = a1[:, :LANES] / a1[:, LANES:]
    return jnp.where(lane < HEAD_DIM, o0, o1)


def _mla_kernel(q_ref, k_ref, v_ref, o_ref, m_ref, acc_ref):
    qi = pl.program_id(2)
    tq = q_ref.shape[0]
    sub = tq // MLA_SPLIT
    _init_state(m_ref, acc_ref)

    def scores(hh, r, c):
        q = q_ref[r * sub:(r + 1) * sub, hh * LANES:(hh + 1) * LANES]
        k = k_ref[pl.ds(pl.multiple_of(c * tq, tq), tq), hh * LANES:(hh + 1) * LANES]
        return _dot_nt(q, k)

    def body(c, carry):
        v = _with_ones(v_ref[pl.ds(pl.multiple_of(c * tq, tq), tq), :])
        for r in range(MLA_SPLIT):
            for hh in range(2):
                _online_update(scores(hh, r, c), v, m_ref, acc_ref, (hh, slice(r * sub, (r + 1) * sub)))
        return carry

    lax.fori_loop(0, qi, body, 0)
    row = lax.broadcasted_iota(I32, (sub, tq), 0)
    col = lax.broadcasted_iota(I32, (sub, tq), 1)
    v = _with_ones(v_ref[pl.ds(pl.multiple_of(qi * tq, tq), tq), :])
    for r in range(MLA_SPLIT):
        for hh in range(2):
            s = jnp.where(col <= row + r * sub, scores(hh, r, qi), MASKED)
            _online_update(s, v, m_ref, acc_ref, (hh, slice(r * sub, (r + 1) * sub)))
    o_ref[...] = _pair_output(acc_ref, 0, 1)


def _mla_call(q, k, v):
    bsz, s_len, _ = q.shape
    tq = MLA_TQ
    npair = A_HEADS // 2
    return pl.pallas_call(
        _mla_kernel,
        grid=(bsz, npair, s_len // tq),
        in_specs=[pl.BlockSpec((None, tq, 2 * LANES), lambda b, p, i: (b, i, p)),
                  pl.BlockSpec((None, s_len, 2 * LANES), lambda b, p, i: (b, 0, p)),
                  pl.BlockSpec((None, s_len, LANES), lambda b, p, i: (b, 0, p))],
        out_specs=pl.BlockSpec((None, tq, LANES), lambda b, p, i: (b, i, p)),
        out_shape=jax.ShapeDtypeStruct((bsz, s_len, A_WIDTH), F32),
        scratch_shapes=[pltpu.VMEM((2, tq, LANES), F32), pltpu.VMEM((2, tq, 2 * LANES), F32)],
        compiler_params=_cparams(("parallel", "parallel", "arbitrary")),
    )(q, k, v)


def _dil_kernel(q_ref, k_ref, v_ref, bias_ref, o_ref, lse_ref):
    i = pl.program_id(2)
    n = DIL_N
    blocks = q_ref.shape[0] // n
    lane = lax.broadcasted_iota(I32, (2 * n, LANES), 1)
    first = lax.broadcasted_iota(I32, (n, LANES), 1) < HEAD_DIM
    for blk in range(blocks):
        j = i * blocks + blk
        start = pl.multiple_of(jnp.maximum(j - 1, 0) * n, n)
        table = jnp.minimum(j, 1)
        for p in range(B_HEADS // 2):
            cols = slice(p * LANES, (p + 1) * LANES)
            q = q_ref[blk * n:(blk + 1) * n, cols]
            k = k_ref[pl.ds(start, 2 * n), cols]
            v = _with_ones(v_ref[pl.ds(start, 2 * n), cols])
            outs, lses = [], []
            for hh in range(2):
                h = 2 * p + hh
                own = (lane >= hh * HEAD_DIM) & (lane < (hh + 1) * HEAD_DIM)
                s = _dot_nt(q, jnp.where(own, k, jnp.zeros_like(k))) + bias_ref[table, h]
                m = jnp.max(s, axis=1, keepdims=True)
                oe = _dot(jnp.exp2(s - m).astype(BF16), v)
                den = oe[:, LANES:]
                outs.append(oe[:, :LANES] / den)
                lses.append(m + jnp.log2(den))
            o_ref[blk * n:(blk + 1) * n, cols] = jnp.where(first, outs[0], outs[1])
            lse_ref[blk * n:(blk + 1) * n, cols] = jnp.where(first, lses[0], lses[1])


def _dil_bias(dil, slopes_b):
    n = DIL_N
    kj = jnp.arange(2 * n)
    step = jnp.arange(n)[:, None] + n - kj[None, :]
    valid = (step >= 0) & (step <= n)
    bias = -(slopes_b * LOG2E)[:, None, None] * (step * dil).astype(F32)[None]
    banded = jnp.where(valid[None], bias, MASKED)
    first = jnp.concatenate([banded[:, :, n:], jnp.full_like(banded[:, :, n:], MASKED)], axis=2)
    return jnp.stack([first, banded])


def _dil_call(q, k, v, dil, slopes_b):
    bsz, s_len, w = q.shape
    L = s_len // dil
    tq = min(DIL_TQ, L)
    view = lambda a: a.reshape(bsz, L, dil * w)
    bias = _dil_bias(dil, slopes_b)
    o, lse = pl.pallas_call(
        _dil_kernel,
        grid=(bsz, dil, L // tq),
        in_specs=[pl.BlockSpec((None, tq, w), lambda b, r, i: (b, i, r)),
                  pl.BlockSpec((None, L, w), lambda b, r, i: (b, 0, r)),
                  pl.BlockSpec((None, L, w), lambda b, r, i: (b, 0, r)),
                  pl.BlockSpec(bias.shape, lambda b, r, i: (0, 0, 0, 0))],
        out_specs=[pl.BlockSpec((None, tq, w), lambda b, r, i: (b, i, r))] * 2,
        out_shape=[jax.ShapeDtypeStruct((bsz, L, dil * w), F32)] * 2,
        compiler_params=_cparams(("parallel", "parallel", "arbitrary")),
    )(view(q), view(k), view(v), bias)
    return o.reshape(bsz, s_len, w), lse.reshape(bsz, s_len, w)


def _sortable(x):
    b = lax.bitcast_convert_type(x, I32)
    return b ^ ((b >> 31) & 0x7FFFFFFF)


def _unsortable(k):
    return lax.bitcast_convert_type(k ^ ((k >> 31) & 0x7FFFFFFF), F32)


def _dsa_kernel(qi_ref, ki_ref, wi_ref, qc_ref, kc_ref, vc_ref, tri_ref, ztab_ref, wtab_ref, slope_ref, o_ref,
                sc_ref, sel_ref, ist_ref, fst_ref, step_ref, m_ref, acc_ref):
    qb = pl.program_id(1)
    nq, ch = DSA_Q, DSA_CH
    q0 = qb * nq
    nk = (q0 + nq + ch - 1) // ch
    row = lax.broadcasted_iota(I32, (nq, ch), 0)
    col = lax.broadcasted_iota(I32, (nq, ch), 1)
    rel = col - row

    def fold(a):
        part = a[:, :LANES]
        for t in range(1, ch // LANES):
            part = part + a[:, t * LANES:(t + 1) * LANES]
        return part

    w = wi_ref[...]

    def score_body(c, carry):
        s1, s2 = carry
        ks = pl.ds(pl.multiple_of(c * ch, ch), ch)
        k_lo = ki_ref[ks, :LANES]
        k_hi = ki_ref[ks, LANES:]
        sc = jnp.zeros((nq, ch), F32)
        for j in range(IDX_HEADS // 2):
            qj = qi_ref[:, j * LANES:(j + 1) * LANES]
            sc = sc + w[:, 2 * j:2 * j + 1] * jnp.maximum(_dot_nt(qj, k_lo), 0.0)
            sc = sc + w[:, 2 * j + 1:2 * j + 2] * jnp.maximum(_dot_nt(qj, k_hi), 0.0)
        causal = rel <= q0 - c * ch
        sc_ref[c] = jnp.where(causal, sc, -jnp.inf)
        kept = jnp.where(causal, sc, 0.0)
        return s1 + fold(kept), s2 + fold(kept * kept)

    zeros = jnp.zeros((nq, LANES), F32)
    s1, s2 = lax.fori_loop(0, nk, score_body, (zeros, zeros))

    def wide(a):
        return jnp.tile(a, (1, ch // LANES))

    def count(pred):
        def body(c, acc):
            return acc + fold(jnp.where(pred(sc_ref[c]), 1.0, 0.0))
        total = jnp.sum(lax.fori_loop(0, nk, body, zeros), axis=1, keepdims=True)
        return jnp.broadcast_to(total, (nq, LANES))

    topk = float(TOPK_MAX)
    target = topk + 0.5
    T_KEY, LO, HI = 0, 1, 2
    C_LO, C_HI, REAL_LO, REAL_HI, THR, C_THR, DONE = range(7)
    fst_ref[THR] = jnp.full((nq, LANES), -FLT_MAX, F32)
    fst_ref[C_THR] = zeros

    @pl.when(q0 + 1 > TOPK_MAX)
    def _():
        n = (q0 + 1 + lax.broadcasted_iota(I32, (nq, LANES), 0)).astype(F32)
        mu = jnp.sum(s1, axis=1, keepdims=True) / n
        var = jnp.maximum(jnp.sum(s2, axis=1, keepdims=True) / n - mu * mu, 0.0)
        sigma = jnp.sqrt(var)
        step_ref[...] = wtab_ref[...] * sigma * 1.5
        lo0 = jnp.full((nq, LANES), _KEY_NEG_MAX, I32)
        hi0 = jnp.full((nq, LANES), _KEY_FIRST_NAN, I32)
        ist_ref[LO] = lo0
        ist_ref[HI] = hi0
        ist_ref[T_KEY] = jnp.clip(_sortable(mu + ztab_ref[...] * sigma), lo0 + 1, hi0 - 1)
        fst_ref[C_LO] = n
        for r in (C_HI, REAL_LO, REAL_HI, DONE):
            fst_ref[r] = zeros

        def cond(st):
            return jnp.logical_and(st[0] < 56, st[1] > 0.0)

        def body(st):
            it = st[0]
            t_key, lo, hi = ist_ref[T_KEY], ist_ref[LO], ist_ref[HI]
            c_lo, c_hi, done = fst_ref[C_LO], fst_ref[C_HI], fst_ref[DONE]
            t = _unsortable(t_key)
            tw = wide(t)
            c = count(lambda s: s >= tw)
            live = done == 0.0
            hit = live & (c == topk)
            up = live & (c > topk)
            dn = live & (c < topk)
            lo = jnp.where(up, t_key, lo)
            c_lo = jnp.where(up, c, c_lo)
            real_lo = jnp.where(up, 1.0, fst_ref[REAL_LO])
            hi = jnp.where(dn, t_key, hi)
            c_hi = jnp.where(dn, c, c_hi)
            real_hi = jnp.where(dn, 1.0, fst_ref[REAL_HI])
            gap = lax.shift_right_logical(hi - lo, 1)
            shut = (gap == 0) & live & jnp.logical_not(hit)
            fst_ref[THR] = jnp.where(hit, t, jnp.where(shut, _unsortable(lo), fst_ref[THR]))
            fst_ref[C_THR] = jnp.where(hit, c, jnp.where(shut, c_lo, fst_ref[C_THR]))
            done = jnp.where(hit | shut, 1.0, done)
            lo_f, hi_f = _unsortable(lo), _unsortable(hi)
            both = real_lo * real_hi
            interp = lo_f + (hi_f - lo_f) * ((c_lo - target) / (c_lo - c_hi))
            guess = jnp.where(both > 0.0, interp, t + step_ref[...] * (c - target))
            g_key = jnp.clip(_sortable(guess), lo + 1, hi - 1)
            every_third = jnp.where(it % 3 == 2, 1.0, 0.0)
            late = jnp.where(it >= 16, 1.0, 0.0)
            bisect = both * every_third + late
            ist_ref[T_KEY] = jnp.where(bisect > 0.0, lo + gap, g_key)
            ist_ref[LO] = lo
            ist_ref[HI] = hi
            fst_ref[C_LO] = c_lo
            fst_ref[C_HI] = c_hi
            fst_ref[REAL_LO] = real_lo
            fst_ref[REAL_HI] = real_hi
            fst_ref[DONE] = done
            return it + 1, float(nq) - jnp.sum(done[:, :1])

        lax.while_loop(cond, body, (jnp.int32(0), jnp.float32(nq)))

    thr = wide(fst_ref[THR])
    has_tie = jnp.max(fst_ref[C_THR]) > topk

    @pl.when(jnp.logical_not(has_tie))
    def _():
        def body(c, carry):
            sel_ref[c] = jnp.where(sc_ref[c] >= thr, 0.0, MASKED)
            return carry
        lax.fori_loop(0, nk, body, 0)

    @pl.when(has_tie)
    def _():
        need = wide(topk - count(lambda s: s > thr))

        def body(c, seen):
            s = sc_ref[c]
            eq = s == thr
            rank = wide(seen) + _dot(jnp.where(eq, 1.0, 0.0).astype(BF16), tri_ref[...])
            take = (s > thr) | (eq & (rank <= need))
            sel_ref[c] = jnp.where(take, 0.0, MASKED)
            return jnp.broadcast_to(rank[:, ch - 1:ch], (nq, LANES))
        lax.fori_loop(0, nk, body, zeros)

    _init_state(m_ref, acc_ref)
    lane = lax.broadcasted_iota(I32, (ch, LANES), 1)
    relf = rel.astype(F32)

    def attn_body(c, carry):
        ks = pl.ds(pl.multiple_of(c * ch, ch), ch)
        dist = relf + (c * ch - q0).astype(F32)
        sel = sel_ref[c]
        for p in range(C_HEADS // 2):
            cols = slice(p * LANES, (p + 1) * LANES)
            q = qc_ref[:, cols]
            k = kc_ref[ks, cols]
            v = _with_ones(vc_ref[ks, cols])
            for hh in range(2):
                h = 2 * p + hh
                own = (lane >= hh * HEAD_DIM) & (lane < (hh + 1) * HEAD_DIM)
                s = _dot_nt(q, jnp.where(own, k, jnp.zeros_like(k)))
                _online_update(s + (slope_ref[h] * dist + sel), v, m_ref, acc_ref, h)
        return carry

    lax.fori_loop(0, nk, attn_body, 0)
    for p in range(C_HEADS // 2):
        o_ref[:, p * LANES:(p + 1) * LANES] = _pair_output(acc_ref, 2 * p, 2 * p + 1)


def _quantile_table(s_len):
    n = jnp.arange(1, s_len + 1, dtype=F32)
    tail = jnp.clip((TOPK_MAX + 0.5) / n, 1e-6, 1.0 - 1e-6)
    z = jax.scipy.special.ndtri(1.0 - tail)
    pdf = jnp.exp(-0.5 * z * z) * (2.0 * np.pi) ** -0.5
    rep = lambda a: jnp.broadcast_to(a[:, None], (s_len, LANES))
    return rep(z), rep(1.0 / (n * pdf))


def _dsa_call(qi, ki, wi, qc, kc, vc, slopes_c):
    bsz, s_len, _ = qc.shape
    nq, ch = DSA_Q, DSA_CH
    tri = (jnp.arange(ch)[:, None] <= jnp.arange(ch)[None, :]).astype(BF16)
    ztab, wtab = _quantile_table(s_len)
    blk = lambda w: pl.BlockSpec((None, nq, w), lambda b, i: (b, i, 0))
    seq = lambda w: pl.BlockSpec((None, s_len, w), lambda b, i: (b, 0, 0))
    return pl.pallas_call(
        _dsa_kernel,
        grid=(bsz, s_len // nq),
        in_specs=[blk(IDX_HEADS * IDX_DIM), seq(2 * LANES), blk(LANES),
                  blk(C_WIDTH), seq(C_WIDTH), seq(C_WIDTH),
                  pl.BlockSpec((ch, ch), lambda b, i: (0, 0)),
                  pl.BlockSpec((nq, LANES), lambda b, i: (i, 0)),
                  pl.BlockSpec((nq, LANES), lambda b, i: (i, 0)),
                  pl.BlockSpec(memory_space=pltpu.SMEM)],
        out_specs=blk(C_WIDTH),
        out_shape=jax.ShapeDtypeStruct((bsz, s_len, C_WIDTH), F32),
        scratch_shapes=[pltpu.VMEM((s_len // ch, nq, ch), F32),
                        pltpu.VMEM((s_len // ch, nq, ch), F32),
                        pltpu.VMEM((3, nq, LANES), I32),
                        pltpu.VMEM((7, nq, LANES), F32),
                        pltpu.VMEM((nq, LANES), F32),
                        pltpu.VMEM((C_HEADS, nq, LANES), F32),
                        pltpu.VMEM((C_HEADS, nq, 2 * LANES), F32)],
        compiler_params=_cparams(("parallel", "arbitrary")),
    )(qi, ki, wi, qc, kc, vc, tri, ztab, wtab, slopes_c * LOG2E)


def _out_kernel(alpha, x_ref, mod_ref, oa_ref, ob1_ref, ob2_ref, ob3_ref, l1_ref, l2_ref, l3_ref,
                oc_ref, gate_ref, wo_ref, g_ref, b_ref, y_ref):
    l1, l2, l3 = l1_ref[...], l2_ref[...], l3_ref[...]
    mx = jnp.maximum(jnp.maximum(l1, l2), l3)
    e1, e2, e3 = jnp.exp2(l1 - mx), jnp.exp2(l2 - mx), jnp.exp2(l3 - mx)
    ob = (e1 * ob1_ref[...] + e2 * ob2_ref[...] + e3 * ob3_ref[...]) / (e1 + e2 + e3)
    pg = gate_ref[...]
    sg = pg * jax.nn.sigmoid(pg)
    ya = (oa_ref[...] * sg[:, :A_WIDTH]).astype(BF16)
    yb = (ob * sg[:, A_WIDTH:A_WIDTH + B_WIDTH]).astype(BF16)
    yc = (oc_ref[...] * sg[:, A_WIDTH + B_WIDTH:]).astype(BF16)
    sub = (_dot(ya, wo_ref[:A_WIDTH, :]) + _dot(yb, wo_ref[A_WIDTH:A_WIDTH + B_WIDTH, :])
           + _dot(yc, wo_ref[A_WIDTH + B_WIDTH:, :]))
    gate = mod_ref[...][:, 2 * D_MODEL:]
    z = alpha * x_ref[...] + (1.0 + gate) * sub
    mu = jnp.mean(z, axis=-1, keepdims=True)
    zc = z - mu
    var = jnp.mean(zc * zc, axis=-1, keepdims=True)
    y_ref[...] = zc * lax.rsqrt(var + 1e-5) * g_ref[...] + b_ref[...]


def _out_call(alpha, x, mod, oa, obs, lses, oc, gate, wo, g, b):
    bsz, s_len, d = x.shape
    tm = PROJ_ROWS
    rows = lambda w: pl.BlockSpec((None, tm, w), lambda bb, i: (bb, i, 0))
    full = lambda a: pl.BlockSpec(a.shape, lambda bb, i: (0,) * a.ndim)
    return pl.pallas_call(
        functools.partial(_out_kernel, alpha),
        grid=(bsz, s_len // tm),
        in_specs=[rows(d), pl.BlockSpec((None, 1, 3 * d), lambda bb, i: (bb, 0, 0)),
                  rows(A_WIDTH)] + [rows(B_WIDTH)] * 6 + [rows(C_WIDTH), rows(MIX_WIDTH),
                  full(wo), full(g), full(b)],
        out_specs=rows(d),
        out_shape=jax.ShapeDtypeStruct((bsz, s_len, d), F32),
        compiler_params=_cparams(("parallel", "parallel")),
    )(x, mod, oa, *obs, *lses, oc, gate, wo, g, b)


def _pad_cols(a, width):
    return jnp.pad(a, ((0, 0), (0, width - a.shape[1])))


def _swap_halves(a):
    half = a.shape[1] // 2
    return jnp.concatenate([a[:, half:], a[:, :half]], axis=1)


def _layer_weights(w_in, w_uq, w_uk, w_uv, w_out):
    d = w_in.shape[0]
    z = lambda n: jnp.zeros((d, n), w_in.dtype)
    w_kr = w_in[:, O_KR:O_QB]
    w1 = jnp.concatenate([w_in[:, O_CQ:O_KR],
                          z(A_NOPE), w_kr, z(LANES - A_NOPE - A_ROPE),
                          z(A_NOPE), _swap_halves(w_kr), z(LANES - A_NOPE - A_ROPE)], axis=1)
    hq = A_NOPE + A_ROPE
    q1, q2 = [], []
    for h in range(A_HEADS):
        blk = w_uq[:, h * hq:(h + 1) * hq]
        q1.append(_pad_cols(blk, LANES))
        zq = jnp.zeros((w_uq.shape[0], A_NOPE), w_uq.dtype)
        q2.append(_pad_cols(jnp.concatenate([zq, _swap_halves(blk[:, A_NOPE:])], axis=1), LANES))
    wq1 = jnp.concatenate(q1, axis=1)
    wq2 = jnp.concatenate(q2, axis=1)
    wk = jnp.concatenate([_pad_cols(w_uk[:, h * A_NOPE:(h + 1) * A_NOPE], LANES)
                          for h in range(A_HEADS)], axis=1)
    w_ki = w_in[:, O_KI:O_WI]
    w2 = jnp.concatenate([w_in[:, O_QB:O_KI],
                          w_ki, z(LANES - IDX_DIM), z(LANES - IDX_DIM), w_ki,
                          w_in[:, O_WI:O_GATE], z(LANES - IDX_HEADS),
                          w_in[:, O_GATE:]], axis=1)
    bf = lambda a: a.astype(BF16)
    return bf(w1), bf(wq1), bf(wq2), bf(wk), bf(w_uv), bf(w2), bf(w_out)


def _rope_tables(s_len):
    pos = jnp.arange(s_len, dtype=F32)
    freqs = ROPE_THETA ** (-jnp.arange(0, A_ROPE, 2, dtype=F32) / A_ROPE)
    ang = pos[:, None] * freqs[None, :]
    cos, sin = jnp.cos(ang), jnp.sin(ang)
    ones = jnp.ones((s_len, A_NOPE), F32)
    zeros = jnp.zeros((s_len, LANES - A_NOPE - A_ROPE), F32)
    ct = jnp.concatenate([ones, cos, cos, zeros], axis=1)
    st = jnp.concatenate([jnp.zeros((s_len, A_NOPE), F32), -sin, sin, zeros], axis=1)
    return ct, st


def kernel(x, c, w_ada, b_ada, w_in, q_norm_g, kv_norm_g, w_uq, w_uk, w_uv, w_out, ln_g, ln_b):
    bsz, s_len, d = x.shape
    depth = w_ada.shape[0]
    alpha = (2 * depth) ** 0.25
    slopes = 2.0 ** (-8.0 * jnp.arange(1, N_ALIBI + 1, dtype=F32) / N_ALIBI)
    slopes_b, slopes_c = slopes[:B_HEADS], slopes[B_HEADS:]
    ct, st = _rope_tables(s_len)
    mod_all = _mod_call(c, w_ada, b_ada)

    for l in range(depth):
        w1, wq1, wq2, wk, wv, w2, wo = _layer_weights(w_in[l], w_uq[l], w_uk[l], w_uv[l], w_out[l])
        mod = mod_all[l].reshape(bsz, 1, 3 * d)
        qa, ka, va = _proj_mla_call(x, mod, w1, q_norm_g[l].reshape(1, -1), kv_norm_g[l].reshape(1, -1),
                                    wq1, wq2, wk, wv, ct, st)
        qb, kb, vb, qc, kc, vc, qi, ki, wi, gate = _proj_rest_call(x, mod, w2)
        o_a = _mla_call(qa, ka, va)
        obs, lses = [], []
        for window, dil in DILATED_PATTERNS:
            assert window // dil == DIL_N
            o, lse = _dil_call(qb, kb, vb, dil, slopes_b)
            obs.append(o)
            lses.append(lse)
        o_c = _dsa_call(qi, ki, wi, qc, kc, vc, slopes_c)
        x = _out_call(alpha, x, mod, o_a, obs, lses, o_c, gate, wo,
                      ln_g[l].reshape(1, -1), ln_b[l].reshape(1, -1))
    return x
```

```python
import functools

import numpy as np
import jax
import jax.numpy as jnp
from jax import lax
from jax.experimental import pallas as pl
from jax.experimental.pallas import tpu as pltpu

F32 = jnp.float32
BF16 = jnp.bfloat16
I32 = jnp.int32

D_MODEL = 1024
HEAD_DIM = 64
A_HEADS = 6
A_NOPE = 64
A_ROPE = 32
A_V = 64
Q_LORA = 768
KV_LORA = 256
ROPE_THETA = 10000.0
B_HEADS = 6
DILATED_PATTERNS = ((128, 1), (512, 4), (2048, 16))
C_HEADS = 4
IDX_HEADS = 8
IDX_DIM = 64
TOPK_MAX = 256
A_WIDTH = A_HEADS * A_V
B_WIDTH = B_HEADS * HEAD_DIM
C_WIDTH = C_HEADS * HEAD_DIM
MIX_WIDTH = A_WIDTH + B_WIDTH + C_WIDTH
N_ALIBI = B_HEADS + C_HEADS
SPLIT_SIZES = (Q_LORA, KV_LORA, A_ROPE, B_WIDTH, B_WIDTH, B_WIDTH, C_WIDTH, C_WIDTH, C_WIDTH,
               IDX_HEADS * IDX_DIM, IDX_DIM, IDX_HEADS, MIX_WIDTH)
_OFFS = [0] + [int(v) for v in np.cumsum(SPLIT_SIZES)]
(O_CQ, O_CKV, O_KR, O_QB, O_KB, O_VB, O_QC, O_KC, O_VC, O_QI, O_KI, O_WI, O_GATE, _O_END) = _OFFS

LANES = 128
SUBLANES = 8
LOG2E = 1.4426950408889634
MASKED = -1e30
VMEM_LIMIT = 48 * 1024 * 1024

PROJ_ROWS = 512
MLA_SPLIT = 4
MLA_TQ = 512
DIL_N = 128
DIL_TQ = 512
DSA_Q = 256
DSA_ONES = 16
DSA_CH = 512
FLT_MAX = float(np.finfo(np.float32).max)
_KEY_NEG_MAX = int(np.int32(np.uint32(0xFF7FFFFF) ^ np.uint32(0x7FFFFFFF)))
_KEY_FIRST_NAN = 0x7F800001


def _dot(a, b):
    return jnp.dot(a, b, preferred_element_type=F32)


def _dot_nt(a, b):
    return lax.dot_general(a, b, (((1,), (1,)), ((), ())), preferred_element_type=F32)


def _cparams(sem):
    return pltpu.CompilerParams(dimension_semantics=sem, vmem_limit_bytes=VMEM_LIMIT)


def _mod_kernel(c_ref, w_ref, b_ref, o_ref):
    c = c_ref[...]
    sc = (c * jax.nn.sigmoid(c)).astype(BF16)
    o_ref[0] = _dot(sc, w_ref[0].astype(BF16)) + b_ref[0]


def _mod_call(c, w_ada, b_ada):
    depth, d, n3 = w_ada.shape
    bsz = c.shape[0]
    tn = 1024
    return pl.pallas_call(
        _mod_kernel,
        grid=(depth, n3 // tn),
        in_specs=[pl.BlockSpec((bsz, d), lambda l, j: (0, 0)),
                  pl.BlockSpec((1, d, tn), lambda l, j: (l, 0, j)),
                  pl.BlockSpec((1, 1, tn), lambda l, j: (l, 0, j))],
        out_specs=pl.BlockSpec((1, bsz, tn), lambda l, j: (l, 0, j)),
        out_shape=jax.ShapeDtypeStruct((depth, bsz, n3), F32),
        compiler_params=_cparams(("arbitrary", "arbitrary")),
    )(c, w_ada, b_ada.reshape(depth, 1, n3))


def _modulate(x_ref, mod_ref):
    mod = mod_ref[...]
    shift = mod[:, :D_MODEL]
    scale = mod[:, D_MODEL:2 * D_MODEL]
    return (x_ref[...] * (1.0 + scale) + shift).astype(BF16)


def _rms(x, g, eps=1e-6):
    ms = jnp.mean(x * x, axis=-1, keepdims=True)
    return (x * lax.rsqrt(ms + eps) * g).astype(BF16)


def _proj_mla_kernel(x_ref, mod_ref, w1_ref, gq_ref, gkv_ref, wq1_ref, wq2_ref, wk_ref, wv_ref,
                     ct_ref, st_ref, q_ref, k_ref, v_ref):
    h = _modulate(x_ref, mod_ref)
    pa = _dot(h, w1_ref[...])
    cq = _rms(pa[:, :Q_LORA], gq_ref[...])
    ckv = _rms(pa[:, Q_LORA:Q_LORA + KV_LORA], gkv_ref[...])
    ct = ct_ref[...]
    st = st_ref[...]
    ct6 = jnp.concatenate([ct] * A_HEADS, axis=1)
    st6 = jnp.concatenate([st] * A_HEADS, axis=1)
    qscale = (A_NOPE + A_ROPE) ** -0.5 * LOG2E
    q = (_dot(cq, wq1_ref[...]) * ct6 + _dot(cq, wq2_ref[...]) * st6) * qscale
    q_ref[...] = q.astype(BF16)
    o = Q_LORA + KV_LORA
    kr = pa[:, o:o + LANES] * ct + pa[:, o + LANES:o + 2 * LANES] * st
    k = _dot(ckv, wk_ref[...]) + jnp.concatenate([kr] * A_HEADS, axis=1)
    k_ref[...] = k.astype(BF16)
    v_ref[...] = _dot(ckv, wv_ref[...]).astype(BF16)


def _proj_mla_call(x, mod, w1, gq, gkv, wq1, wq2, wk, wv, ct, st):
    bsz, s_len, d = x.shape
    tm = PROJ_ROWS
    hw = A_HEADS * LANES
    full = lambda a: pl.BlockSpec(a.shape, lambda b, i: (0,) * a.ndim)
    return pl.pallas_call(
        _proj_mla_kernel,
        grid=(bsz, s_len // tm),
        in_specs=[pl.BlockSpec((None, tm, d), lambda b, i: (b, i, 0)),
                  pl.BlockSpec((None, 1, 3 * d), lambda b, i: (b, 0, 0)),
                  full(w1), full(gq), full(gkv), full(wq1), full(wq2), full(wk), full(wv),
                  pl.BlockSpec((tm, LANES), lambda b, i: (i, 0)),
                  pl.BlockSpec((tm, LANES), lambda b, i: (i, 0))],
        out_specs=[pl.BlockSpec((None, tm, hw), lambda b, i: (b, i, 0)),
                   pl.BlockSpec((None, tm, hw), lambda b, i: (b, i, 0)),
                   pl.BlockSpec((None, tm, A_WIDTH), lambda b, i: (b, i, 0))],
        out_shape=[jax.ShapeDtypeStruct((bsz, s_len, hw), BF16),
                   jax.ShapeDtypeStruct((bsz, s_len, hw), BF16),
                   jax.ShapeDtypeStruct((bsz, s_len, A_WIDTH), BF16)],
        compiler_params=_cparams(("parallel", "parallel")),
    )(x, mod, w1, gq, gkv, wq1, wq2, wk, wv, ct, st)


_R_QB, _R_KB, _R_VB = 0, B_WIDTH, 2 * B_WIDTH
_R_QC = 3 * B_WIDTH
_R_KC, _R_VC = _R_QC + C_WIDTH, _R_QC + 2 * C_WIDTH
_R_QI = _R_QC + 3 * C_WIDTH
_R_KI = _R_QI + IDX_HEADS * IDX_DIM
_R_WI = _R_KI + 2 * LANES
_R_GATE = _R_WI + LANES
_R_END = _R_GATE + MIX_WIDTH


def _proj_rest_kernel(x_ref, mod_ref, w2_ref, wvt_ref, qb_ref, kb_ref, vb_ref, qc_ref, kc_ref, vct_ref,
                      qi_ref, ki_ref, wi_ref, gate_ref):
    h = _modulate(x_ref, mod_ref)

    def proj(start, width):
        return _dot(h, w2_ref[:, start:start + width])

    inv_sqrt_dh = HEAD_DIM ** -0.5 * LOG2E
    qb_ref[...] = (proj(_R_QB, B_WIDTH) * inv_sqrt_dh).astype(BF16)
    kb_ref[...] = proj(_R_KB, B_WIDTH).astype(BF16)
    vb_ref[...] = proj(_R_VB, B_WIDTH).astype(BF16)
    qc_ref[...] = (proj(_R_QC, C_WIDTH) * inv_sqrt_dh).astype(BF16)
    kc_ref[...] = proj(_R_KC, C_WIDTH).astype(BF16)
    vct_ref[...] = _dot_nt(wvt_ref[...], h).astype(BF16)
    qi_ref[...] = (proj(_R_QI, IDX_HEADS * IDX_DIM) * IDX_DIM ** -0.5).astype(BF16)
    ki_ref[...] = proj(_R_KI, 2 * LANES).astype(BF16)
    wi_ref[...] = proj(_R_WI, LANES) * IDX_HEADS ** -0.5
    gate_ref[...] = proj(_R_GATE, MIX_WIDTH)


def _proj_rest_call(x, mod, w2, wv_t):
    bsz, s_len, d = x.shape
    tm = PROJ_ROWS
    widths = (B_WIDTH, B_WIDTH, B_WIDTH, C_WIDTH, C_WIDTH, C_WIDTH, IDX_HEADS * IDX_DIM,
              2 * LANES, LANES, MIX_WIDTH)
    dtypes = (BF16,) * 8 + (F32, F32)
    return pl.pallas_call(
        _proj_rest_kernel,
        grid=(bsz, s_len // tm),
        in_specs=[pl.BlockSpec((None, tm, d), lambda b, i: (b, i, 0)),
                  pl.BlockSpec((None, 1, 3 * d), lambda b, i: (b, 0, 0)),
                  pl.BlockSpec(w2.shape, lambda b, i: (0, 0)),
                  pl.BlockSpec(wv_t.shape, lambda b, i: (0, 0))],
        out_specs=[pl.BlockSpec((None, C_WIDTH, tm), lambda b, i: (b, 0, i)) if n == 5 else
                   pl.BlockSpec((None, tm, w), lambda b, i: (b, i, 0)) for n, w in enumerate(widths)],
        out_shape=[jax.ShapeDtypeStruct((bsz, C_WIDTH, s_len) if n == 5 else (bsz, s_len, w), dt)
                   for n, (w, dt) in enumerate(zip(widths, dtypes))],
        compiler_params=_cparams(("parallel", "parallel")),
    )(x, mod, w2, wv_t)


def _online_update(s, v_ext, m_ref, acc_ref, idx):
    m_prev = m_ref[idx]
    m_new = jnp.maximum(m_prev, jnp.max(s, axis=1, keepdims=True))
    alpha = jnp.exp2(m_prev - m_new)
    p = jnp.exp2(s - jnp.tile(m_new, (1, s.shape[1] // LANES)))
    acc_ref[idx] = jnp.tile(alpha, (1, 2)) * acc_ref[idx] + _dot(p.astype(BF16), v_ext)
    m_ref[idx] = m_new


def _with_ones(v):
    return jnp.concatenate([v, jnp.ones_like(v)], axis=1)


def _init_state(m_ref, acc_ref):
    m_ref[...] = jnp.full(m_ref.shape, MASKED, F32)
    acc_ref[...] = jnp.zeros(acc_ref.shape, F32)


def _pair_output(acc_ref, i0, i1):
    lane = lax.broadcasted_iota(I32, (acc_ref.shape[1], LANES), 1)
    a0, a1 = acc_ref[i0], acc_ref[i1]
    o0 = a0[:, :LANES] / a0[:, LANES:]
    o1 = a1[:, :LANES] / a1[:, LANES:]
    return jnp.where(lane < HEAD_DIM, o0, o1)


def _mla_kernel(q_ref, k_ref, v_ref, o_ref, m_ref, acc_ref):
    qi = pl.program_id(2)
    tq = q_ref.shape[0]
    sub = tq // MLA_SPLIT
    _init_state(m_ref, acc_ref)

    def scores(hh, r, c):
        q = q_ref[r * sub:(r + 1) * sub, hh * LANES:(hh + 1) * LANES]
        k = k_ref[pl.ds(pl.multiple_of(c * tq, tq), tq), hh * LANES:(hh + 1) * LANES]
        return _dot_nt(q, k)

    def body(c, carry):
        v = _with_ones(v_ref[pl.ds(pl.multiple_of(c * tq, tq), tq), :])
        for r in range(MLA_SPLIT):
            for hh in range(2):
                _online_update(scores(hh, r, c), v, m_ref, acc_ref, (hh, slice(r * sub, (r + 1) * sub)))
        return carry

    lax.fori_loop(0, qi, body, 0)
    row = lax.broadcasted_iota(I32, (sub, tq), 0)
    col = lax.broadcasted_iota(I32, (sub, tq), 1)
    v = _with_ones(v_ref[pl.ds(pl.multiple_of(qi * tq, tq), tq), :])
    for r in range(MLA_SPLIT):
        for hh in range(2):
            s = jnp.where(col <= row + r * sub, scores(hh, r, qi), MASKED)
            _online_update(s, v, m_ref, acc_ref, (hh, slice(r * sub, (r + 1) * sub)))
    o_ref[...] = _pair_output(acc_ref, 0, 1)


def _mla_call(q, k, v):
    bsz, s_len, _ = q.shape
    tq = MLA_TQ
    npair = A_HEADS // 2
    return pl.pallas_call(
        _mla_kernel,
        grid=(bsz, npair, s_len // tq),
        in_specs=[pl.BlockSpec((None, tq, 2 * LANES), lambda b, p, i: (b, i, p)),
                  pl.BlockSpec((None, s_len, 2 * LANES), lambda b, p, i: (b, 0, p)),
                  pl.BlockSpec((None, s_len, LANES), lambda b, p, i: (b, 0, p))],
        out_specs=pl.BlockSpec((None, tq, LANES), lambda b, p, i: (b, i, p)),
        out_shape=jax.ShapeDtypeStruct((bsz, s_len, A_WIDTH), F32),
        scratch_shapes=[pltpu.VMEM((2, tq, LANES), F32), pltpu.VMEM((2, tq, 2 * LANES), F32)],
        compiler_params=_cparams(("parallel", "parallel", "arbitrary")),
    )(q, k, v)


def _dil_kernel(q_ref, k_ref, v_ref, bias_ref, o_ref, lse_ref):
    i = pl.program_id(2)
    n = DIL_N
    blocks = q_ref.shape[0] // n
    lane = lax.broadcasted_iota(I32, (2 * n, LANES), 1)
    first = lax.broadcasted_iota(I32, (n, LANES), 1) < HEAD_DIM
    for blk in range(blocks):
        j = i * blocks + blk
        start = pl.multiple_of(jnp.maximum(j - 1, 0) * n, n)
        table = jnp.minimum(j, 1)
        for p in range(B_HEADS // 2):
            cols = slice(p * LANES, (p + 1) * LANES)
            q = q_ref[blk * n:(blk + 1) * n, cols]
            k = k_ref[pl.ds(start, 2 * n), cols]
            v = _with_ones(v_ref[pl.ds(start, 2 * n), cols])
            outs, lses = [], []
            for hh in range(2):
                h = 2 * p + hh
                own = (lane >= hh * HEAD_DIM) & (lane < (hh + 1) * HEAD_DIM)
                s = _dot_nt(q, jnp.where(own, k, jnp.zeros_like(k))) + bias_ref[table, h]
                m = jnp.max(s, axis=1, keepdims=True)
                oe = _dot(jnp.exp2(s - m).astype(BF16), v)
                den = oe[:, LANES:]
                outs.append(oe[:, :LANES] / den)
                lses.append(m + jnp.log2(den))
            o_ref[blk * n:(blk + 1) * n, cols] = jnp.where(first, outs[0], outs[1])
            lse_ref[blk * n:(blk + 1) * n, cols] = jnp.where(first, lses[0], lses[1])


def _dil_bias(dil, slopes_b):
    n = DIL_N
    kj = jnp.arange(2 * n)
    step = jnp.arange(n)[:, None] + n - kj[None, :]
    valid = (step >= 0) & (step <= n)
    bias = -(slopes_b * LOG2E)[:, None, None] * (step * dil).astype(F32)[None]
    banded = jnp.where(valid[None], bias, MASKED)
    first = jnp.concatenate([banded[:, :, n:], jnp.full_like(banded[:, :, n:], MASKED)], axis=2)
    return jnp.stack([first, banded])


def _dil_call(q, k, v, dil, slopes_b):
    bsz, s_len, w = q.shape
    L = s_len // dil
    tq = min(DIL_TQ, L)
    view = lambda a: a.reshape(bsz, L, dil * w)
    bias = _dil_bias(dil, slopes_b)
    o, lse = pl.pallas_call(
        _dil_kernel,
        grid=(bsz, dil, L // tq),
        in_specs=[pl.BlockSpec((None, tq, w), lambda b, r, i: (b, i, r)),
                  pl.BlockSpec((None, L, w), lambda b, r, i: (b, 0, r)),
                  pl.BlockSpec((None, L, w), lambda b, r, i: (b, 0, r)),
                  pl.BlockSpec(bias.shape, lambda b, r, i: (0, 0, 0, 0))],
        out_specs=[pl.BlockSpec((None, tq, w), lambda b, r, i: (b, i, r))] * 2,
        out_shape=[jax.ShapeDtypeStruct((bsz, L, dil * w), F32)] * 2,
        compiler_params=_cparams(("parallel", "parallel", "arbitrary")),
    )(view(q), view(k), view(v), bias)
    return o.reshape(bsz, s_len, w), lse.reshape(bsz, s_len, w)


def _sortable(x):
    b = lax.bitcast_convert_type(x, I32)
    return b ^ ((b >> 31) & 0x7FFFFFFF)


def _unsortable(k):
    return lax.bitcast_convert_type(k ^ ((k >> 31) & 0x7FFFFFFF), F32)


def _dsa_kernel(qi_ref, ki_ref, wi_ref, qc_ref, kc_ref, vt_ref, tri_ref, ztab_ref, wtab_ref, slope_ref, o_ref,
                sc_ref, sel_ref, ist_ref, fst_ref, m_ref, acc_ref):
    qb = pl.program_id(1)
    nq, ch = DSA_Q, DSA_CH
    g = ch // SUBLANES
    q0 = qb * nq
    nk = (q0 + nq + ch - 1) // ch
    key_i = lax.broadcasted_iota(I32, (ch, nq), 0)
    qry_i = lax.broadcasted_iota(I32, (ch, nq), 1)
    rel = key_i - qry_i

    def rows(a):
        return jnp.broadcast_to(a[:1], (SUBLANES, nq))

    def groups(a):
        return a.reshape(g, SUBLANES, nq)

    def gsum(a):
        return jnp.sum(jnp.sum(a.reshape(SUBLANES, g // SUBLANES, SUBLANES, nq), axis=1), axis=0)

    def gmax(a):
        return jnp.max(jnp.max(a.reshape(SUBLANES, g // SUBLANES, SUBLANES, nq), axis=1), axis=0)

    def colsum(a):
        return rows(jnp.sum(gsum(groups(a)), axis=0, keepdims=True))

    w_t = wi_ref[...].T
    w_rows = [w_t[h:h + 1, :] for h in range(IDX_HEADS)]

    def score_body(c, carry):
        s1, s2 = carry
        ks = pl.ds(pl.multiple_of(c * ch, ch), ch)
        k_lo = ki_ref[ks, :LANES]
        k_hi = ki_ref[ks, LANES:]
        sc = jnp.zeros((ch, nq), F32)
        for j in range(IDX_HEADS // 2):
            qj = qi_ref[:, j * LANES:(j + 1) * LANES]
            sc = sc + w_rows[2 * j] * jnp.maximum(_dot_nt(k_lo, qj), 0.0)
            sc = sc + w_rows[2 * j + 1] * jnp.maximum(_dot_nt(k_hi, qj), 0.0)
        causal = rel <= q0 - c * ch
        sc_ref[c] = groups(jnp.where(causal, sc, -jnp.inf))
        kept = jnp.where(causal, sc, 0.0)
        return s1 + colsum(kept), s2 + colsum(kept * kept)

    zeros = jnp.zeros((SUBLANES, nq), F32)
    s1, s2 = lax.fori_loop(0, nk, score_body, (zeros, zeros))

    def count(*preds):
        def body(c, accs):
            s = sc_ref[c]
            return tuple(a + gsum(jnp.where(p(s), 1.0, 0.0)) for a, p in zip(accs, preds))
        accs = lax.fori_loop(0, nk, body, (zeros,) * len(preds))
        return tuple(rows(jnp.sum(a, axis=0, keepdims=True)) for a in accs)

    topk = float(TOPK_MAX)
    target = topk + 0.5
    T_KEY, LO, HI = 0, 1, 2
    C_LO, C_HI, REAL_LO, REAL_HI, SIDE, THR, C_THR, DONE, STEP = range(9)
    fst_ref[THR] = jnp.full((SUBLANES, nq), -FLT_MAX, F32)
    fst_ref[C_THR] = zeros

    @pl.when(q0 + 1 > TOPK_MAX)
    def _():
        n = (q0 + 1 + lax.broadcasted_iota(I32, (SUBLANES, nq), 1)).astype(F32)
        mu = s1 / n
        sigma = jnp.sqrt(jnp.maximum(s2 / n - mu * mu, 0.0))
        fst_ref[STEP] = wtab_ref[...] * sigma * 1.5
        c_zero, c_pos = count(lambda s: s >= 0.0, lambda s: s > 0.0)
        zero_tie = (c_pos < topk) & (c_zero >= topk)
        above = c_pos >= topk
        below = c_zero < topk
        key0 = jnp.zeros((SUBLANES, nq), I32)
        lo0 = jnp.where(above, key0, jnp.full((SUBLANES, nq), _KEY_NEG_MAX, I32))
        hi0 = jnp.where(below, key0, jnp.full((SUBLANES, nq), _KEY_FIRST_NAN, I32))
        ist_ref[LO] = lo0
        ist_ref[HI] = hi0
        ist_ref[T_KEY] = jnp.clip(_sortable(mu + ztab_ref[...] * sigma), lo0 + 1, hi0 - 1)
        fst_ref[C_LO] = jnp.where(above, c_zero, n)
        fst_ref[C_HI] = jnp.where(below, c_zero, 0.0)
        fst_ref[REAL_LO] = jnp.where(above, 1.0, 0.0)
        fst_ref[REAL_HI] = jnp.where(below, 1.0, 0.0)
        fst_ref[SIDE] = zeros
        fst_ref[THR] = jnp.where(zero_tie, 0.0, -FLT_MAX)
        fst_ref[C_THR] = jnp.where(zero_tie, c_zero, 0.0)
        done0 = jnp.where(zero_tie, 1.0, 0.0)
        fst_ref[DONE] = done0

        def cond(st):
            return jnp.logical_and(st[0] < 60, st[1] > 0.0)

        def body(st):
            it = st[0]
            t_key, lo, hi = ist_ref[T_KEY], ist_ref[LO], ist_ref[HI]
            c_lo, c_hi, done, side = fst_ref[C_LO], fst_ref[C_HI], fst_ref[DONE], fst_ref[SIDE]
            t = _unsortable(t_key)
            (c,) = count(lambda s: s >= t[None])
            live = done == 0.0
            hit = live & (c == topk)
            up = live & (c > topk)
            dn = live & (c < topk)
            moved = jnp.where(up, 1.0, jnp.where(dn, -1.0, 0.0))
            repeat = (moved == side) & (moved != 0.0)
            lo = jnp.where(up, t_key, lo)
            c_lo = jnp.where(up, c, c_lo)
            real_lo = jnp.where(up, 1.0, fst_ref[REAL_LO])
            hi = jnp.where(dn, t_key, hi)
            c_hi = jnp.where(dn, c, c_hi)
            real_hi = jnp.where(dn, 1.0, fst_ref[REAL_HI])
            gap = lax.shift_right_logical(hi - lo, 1)
            shut = (gap == 0) & live & jnp.logical_not(hit)
            fst_ref[THR] = jnp.where(hit, t, jnp.where(shut, _unsortable(lo), fst_ref[THR]))
            fst_ref[C_THR] = jnp.where(hit, c, jnp.where(shut, c_lo, fst_ref[C_THR]))
            done = jnp.where(hit | shut, 1.0, done)
            lo_f, hi_f = _unsortable(lo), _unsortable(hi)
            both = real_lo * real_hi
            frac = jnp.where(repeat, 0.5, (c_lo - target) / (c_lo - c_hi))
            inside = lo_f + (hi_f - lo_f) * frac
            guess = jnp.where(both > 0.0, inside, t + fst_ref[STEP] * (c - target))
            g_key = jnp.clip(_sortable(guess), lo + 1, hi - 1)
            late = jnp.where(it >= 14, 1.0, 0.0)
            ist_ref[T_KEY] = jnp.where(late > 0.0, lo + gap, g_key)
            ist_ref[LO] = lo
            ist_ref[HI] = hi
            fst_ref[C_LO] = c_lo
            fst_ref[C_HI] = c_hi
            fst_ref[REAL_LO] = real_lo
            fst_ref[REAL_HI] = real_hi
            fst_ref[SIDE] = jnp.where(moved != 0.0, moved, side)
            fst_ref[DONE] = done
            return it + 1, float(nq) - jnp.sum(done[:1])

        lax.while_loop(cond, body, (jnp.int32(0), float(nq) - jnp.sum(done0[:1])))

    thr = fst_ref[THR]
    has_tie = jnp.max(fst_ref[C_THR]) > topk

    @pl.when(jnp.logical_not(has_tie))
    def _():
        def body(c, carry):
            sel_ref[c] = jnp.where(sc_ref[c] >= thr[None], 0.0, MASKED)
            return carry
        lax.fori_loop(0, nk, body, 0)

    @pl.when(has_tie)
    def _():
        (n_gt,) = count(lambda s: s > thr[None])
        need = topk - n_gt

        def body(c, seen):
            s = sc_ref[c]
            eq = s == thr[None]
            ones = jnp.where(eq, 1.0, 0.0).reshape(ch, nq).astype(BF16)
            rank = groups(_dot(tri_ref[...], ones)) + seen[None]
            take = (s > thr[None]) | (eq & (rank <= need[None]))
            sel_ref[c] = jnp.where(take, 0.0, MASKED)
            return rows(rank[g - 1, SUBLANES - 1:SUBLANES, :])
        lax.fori_loop(0, nk, body, zeros)

    m_ref[...] = jnp.full(m_ref.shape, MASKED, F32)
    acc_ref[...] = jnp.zeros(acc_ref.shape, F32)
    lane = lax.broadcasted_iota(I32, (ch, LANES), 1)
    relf = rel.astype(F32)
    ones_rows = jnp.ones((DSA_ONES, ch), BF16)
    ag = acc_ref.shape[1] // SUBLANES

    def attn_body(c, carry):
        ks = pl.ds(pl.multiple_of(c * ch, ch), ch)
        dist = relf + (c * ch - q0).astype(F32)
        sel = sel_ref[c].reshape(ch, nq)
        for p in range(C_HEADS // 2):
            cols = slice(p * LANES, (p + 1) * LANES)
            k = kc_ref[ks, cols]
            zero = jnp.zeros_like(k)
            k2 = jnp.concatenate([jnp.where(lane < HEAD_DIM, k, zero), jnp.where(lane >= HEAD_DIM, k, zero)], axis=0)
            s2 = _dot_nt(k2, qc_ref[:, cols])
            prs, alphas = [], []
            for hh in range(2):
                h = 2 * p + hh
                s = groups(s2[hh * ch:(hh + 1) * ch] + (slope_ref[h] * dist + sel))
                m_prev = m_ref[h]
                m_new = jnp.maximum(m_prev, rows(jnp.max(gmax(s), axis=0, keepdims=True)))
                alphas.append(jnp.exp2(m_prev - m_new))
                prs.append(jnp.exp2(s - m_new[None]).reshape(ch, nq).astype(BF16))
                m_ref[h] = m_new
            v_t = jnp.concatenate([vt_ref[cols, ks], ones_rows], axis=0)
            pv = _dot(v_t, jnp.concatenate(prs, axis=1))
            for hh in range(2):
                h = 2 * p + hh
                acc = acc_ref[h].reshape(ag, SUBLANES, nq) * alphas[hh][None]
                acc_ref[h] = acc.reshape(ag * SUBLANES, nq) + pv[:, hh * nq:(hh + 1) * nq]
        return carry

    lax.fori_loop(0, nk, attn_body, 0)
    dim = lax.broadcasted_iota(I32, (LANES, nq), 0)
    for p in range(C_HEADS // 2):
        outs = []
        for hh in range(2):
            acc = acc_ref[2 * p + hh]
            den = acc[LANES:LANES + SUBLANES]
            outs.append((acc[:LANES].reshape(LANES // SUBLANES, SUBLANES, nq) / den[None]).reshape(LANES, nq))
        o_ref[:, p * LANES:(p + 1) * LANES] = jnp.where(dim < HEAD_DIM, outs[0], outs[1]).T


def _quantile_table(s_len):
    n = jnp.arange(1, s_len + 1, dtype=F32)
    tail = jnp.clip((TOPK_MAX + 0.5) / n, 1e-6, 1.0 - 1e-6)
    z = jax.scipy.special.ndtri(1.0 - tail)
    pdf = jnp.exp(-0.5 * z * z) * (2.0 * np.pi) ** -0.5
    rep = lambda a: jnp.broadcast_to(a[None, :], (SUBLANES, s_len))
    return rep(z), rep(1.0 / (n * pdf))


def _dsa_call(qi, ki, wi, qc, kc, vc_t, slopes_c):
    bsz, s_len, _ = qc.shape
    nq, ch = DSA_Q, DSA_CH
    tri = (jnp.arange(ch)[:, None] >= jnp.arange(ch)[None, :]).astype(BF16)
    ztab, wtab = _quantile_table(s_len)
    blk = lambda w: pl.BlockSpec((None, nq, w), lambda b, i: (b, i, 0))
    seq = lambda w: pl.BlockSpec((None, s_len, w), lambda b, i: (b, 0, 0))
    tab = pl.BlockSpec((SUBLANES, nq), lambda b, i: (0, i))
    return pl.pallas_call(
        _dsa_kernel,
        grid=(bsz, s_len // nq),
        in_specs=[blk(IDX_HEADS * IDX_DIM), seq(2 * LANES), blk(LANES),
                  blk(C_WIDTH), seq(C_WIDTH),
                  pl.BlockSpec((None, C_WIDTH, s_len), lambda b, i: (b, 0, 0)),
                  pl.BlockSpec((ch, ch), lambda b, i: (0, 0)),
                  tab, tab,
                  pl.BlockSpec(memory_space=pltpu.SMEM)],
        out_specs=blk(C_WIDTH),
        out_shape=jax.ShapeDtypeStruct((bsz, s_len, C_WIDTH), F32),
        scratch_shapes=[pltpu.VMEM((s_len // ch, ch // SUBLANES, SUBLANES, nq), F32),
                        pltpu.VMEM((s_len // ch, ch // SUBLANES, SUBLANES, nq), F32),
                        pltpu.VMEM((3, SUBLANES, nq), I32),
                        pltpu.VMEM((9, SUBLANES, nq), F32),
                        pltpu.VMEM((C_HEADS, SUBLANES, nq), F32),
                        pltpu.VMEM((C_HEADS, LANES + DSA_ONES, nq), F32)],
        compiler_params=_cparams(("parallel", "arbitrary")),
    )(qi, ki, wi, qc, kc, vc_t, tri, ztab, wtab, slopes_c * LOG2E)


def _out_kernel(alpha, x_ref, mod_ref, oa_ref, ob1_ref, ob2_ref, ob3_ref, l1_ref, l2_ref, l3_ref,
                oc_ref, gate_ref, wo_ref, g_ref, b_ref, y_ref):
    l1, l2, l3 = l1_ref[...], l2_ref[...], l3_ref[...]
    mx = jnp.maximum(jnp.maximum(l1, l2), l3)
    e1, e2, e3 = jnp.exp2(l1 - mx), jnp.exp2(l2 - mx), jnp.exp2(l3 - mx)
    ob = (e1 * ob1_ref[...] + e2 * ob2_ref[...] + e3 * ob3_ref[...]) / (e1 + e2 + e3)
    pg = gate_ref[...]
    sg = pg * jax.nn.sigmoid(pg)
    ya = (oa_ref[...] * sg[:, :A_WIDTH]).astype(BF16)
    yb = (ob * sg[:, A_WIDTH:A_WIDTH + B_WIDTH]).astype(BF16)
    yc = (oc_ref[...] * sg[:, A_WIDTH + B_WIDTH:]).astype(BF16)
    sub = (_dot(ya, wo_ref[:A_WIDTH, :]) + _dot(yb, wo_ref[A_WIDTH:A_WIDTH + B_WIDTH, :])
           + _dot(yc, wo_ref[A_WIDTH + B_WIDTH:, :]))
    gate = mod_ref[...][:, 2 * D_MODEL:]
    z = alpha * x_ref[...] + (1.0 + gate) * sub
    mu = jnp.mean(z, axis=-1, keepdims=True)
    zc = z - mu
    var = jnp.mean(zc * zc, axis=-1, keepdims=True)
    y_ref[...] = zc * lax.rsqrt(var + 1e-5) * g_ref[...] + b_ref[...]


def _out_call(alpha, x, mod, oa, obs, lses, oc, gate, wo, g, b):
    bsz, s_len, d = x.shape
    tm = PROJ_ROWS
    rows = lambda w: pl.BlockSpec((None, tm, w), lambda bb, i: (bb, i, 0))
    full = lambda a: pl.BlockSpec(a.shape, lambda bb, i: (0,) * a.ndim)
    return pl.pallas_call(
        functools.partial(_out_kernel, alpha),
        grid=(bsz, s_len // tm),
        in_specs=[rows(d), pl.BlockSpec((None, 1, 3 * d), lambda bb, i: (bb, 0, 0)),
                  rows(A_WIDTH)] + [rows(B_WIDTH)] * 6 + [rows(C_WIDTH), rows(MIX_WIDTH),
                  full(wo), full(g), full(b)],
        out_specs=rows(d),
        out_shape=jax.ShapeDtypeStruct((bsz, s_len, d), F32),
        compiler_params=_cparams(("parallel", "parallel")),
    )(x, mod, oa, *obs, *lses, oc, gate, wo, g, b)


def _pad_cols(a, width):
    return jnp.pad(a, ((0, 0), (0, width - a.shape[1])))


def _swap_halves(a):
    half = a.shape[1] // 2
    return jnp.concatenate([a[:, half:], a[:, :half]], axis=1)


def _layer_weights(w_in, w_uq, w_uk, w_uv, w_out):
    d = w_in.shape[0]
    z = lambda n: jnp.zeros((d, n), w_in.dtype)
    w_kr = w_in[:, O_KR:O_QB]
    w1 = jnp.concatenate([w_in[:, O_CQ:O_KR],
                          z(A_NOPE), w_kr, z(LANES - A_NOPE - A_ROPE),
                          z(A_NOPE), _swap_halves(w_kr), z(LANES - A_NOPE - A_ROPE)], axis=1)
    hq = A_NOPE + A_ROPE
    q1, q2 = [], []
    for h in range(A_HEADS):
        blk = w_uq[:, h * hq:(h + 1) * hq]
        q1.append(_pad_cols(blk, LANES))
        zq = jnp.zeros((w_uq.shape[0], A_NOPE), w_uq.dtype)
        q2.append(_pad_cols(jnp.concatenate([zq, _swap_halves(blk[:, A_NOPE:])], axis=1), LANES))
    wq1 = jnp.concatenate(q1, axis=1)
    wq2 = jnp.concatenate(q2, axis=1)
    wk = jnp.concatenate([_pad_cols(w_uk[:, h * A_NOPE:(h + 1) * A_NOPE], LANES)
                          for h in range(A_HEADS)], axis=1)
    w_ki = w_in[:, O_KI:O_WI]
    w2 = jnp.concatenate([w_in[:, O_QB:O_KI],
                          w_ki, z(LANES - IDX_DIM), z(LANES - IDX_DIM), w_ki,
                          w_in[:, O_WI:O_GATE], z(LANES - IDX_HEADS),
                          w_in[:, O_GATE:]], axis=1)
    bf = lambda a: a.astype(BF16)
    wv_t = w_in[:, O_VC:O_QI].T
    return bf(w1), bf(wq1), bf(wq2), bf(wk), bf(w_uv), bf(w2), bf(wv_t), bf(w_out)


def _rope_tables(s_len):
    pos = jnp.arange(s_len, dtype=F32)
    freqs = ROPE_THETA ** (-jnp.arange(0, A_ROPE, 2, dtype=F32) / A_ROPE)
    ang = pos[:, None] * freqs[None, :]
    cos, sin = jnp.cos(ang), jnp.sin(ang)
    ones = jnp.ones((s_len, A_NOPE), F32)
    zeros = jnp.zeros((s_len, LANES - A_NOPE - A_ROPE), F32)
    ct = jnp.concatenate([ones, cos, cos, zeros], axis=1)
    st = jnp.concatenate([jnp.zeros((s_len, A_NOPE), F32), -sin, sin, zeros], axis=1)
    return ct, st


def kernel(x, c, w_ada, b_ada, w_in, q_norm_g, kv_norm_g, w_uq, w_uk, w_uv, w_out, ln_g, ln_b):
    bsz, s_len, d = x.shape
    depth = w_ada.shape[0]
    alpha = (2 * depth) ** 0.25
    slopes = 2.0 ** (-8.0 * jnp.arange(1, N_ALIBI + 1, dtype=F32) / N_ALIBI)
    slopes_b, slopes_c = slopes[:B_HEADS], slopes[B_HEADS:]
    ct, st = _rope_tables(s_len)
    mod_all = _mod_call(c, w_ada, b_ada)

    for l in range(depth):
        w1, wq1, wq2, wk, wv, w2, wv_t, wo = _layer_weights(w_in[l], w_uq[l], w_uk[l], w_uv[l], w_out[l])
        mod = mod_all[l].reshape(bsz, 1, 3 * d)
        qa, ka, va = _proj_mla_call(x, mod, w1, q_norm_g[l].reshape(1, -1), kv_norm_g[l].reshape(1, -1),
                                    wq1, wq2, wk, wv, ct, st)
        qb, kb, vb, qc, kc, vc_t, qi, ki, wi, gate = _proj_rest_call(x, mod, w2, wv_t)
        o_a = _mla_call(qa, ka, va)
        obs, lses = [], []
        for window, dil in DILATED_PATTERNS:
            assert window // dil == DIL_N
            o, lse = _dil_call(qb, kb, vb, dil, slopes_b)
            obs.append(o)
            lses.append(lse)
        o_c = _dsa_call(qi, ki, wi, qc, kc, vc_t, slopes_c)
        x = _out_call(alpha, x, mod, o_a, obs, lses, o_c, gate, wo,
                      ln_g[l].reshape(1, -1), ln_b[l].reshape(1, -1))
    return x
```

```python
import functools

import numpy as np
import jax
import jax.numpy as jnp
from jax import lax
from jax.experimental import pallas as pl
from jax.experimental.pallas import tpu as pltpu

F32 = jnp.float32
BF16 = jnp.bfloat16
I32 = jnp.int32

D_MODEL = 1024
HEAD_DIM = 64
A_HEADS = 6
A_NOPE = 64
A_ROPE = 32
A_V = 64
Q_LORA = 768
KV_LORA = 256
ROPE_THETA = 10000.0
B_HEADS = 6
DILATED_PATTERNS = ((128, 1), (512, 4), (2048, 16))
C_HEADS = 4
IDX_HEADS = 8
IDX_DIM = 64
TOPK_MAX = 256
A_WIDTH = A_HEADS * A_V
B_WIDTH = B_HEADS * HEAD_DIM
C_WIDTH = C_HEADS * HEAD_DIM
MIX_WIDTH = A_WIDTH + B_WIDTH + C_WIDTH
N_ALIBI = B_HEADS + C_HEADS
SPLIT_SIZES = (Q_LORA, KV_LORA, A_ROPE, B_WIDTH, B_WIDTH, B_WIDTH, C_WIDTH, C_WIDTH, C_WIDTH,
               IDX_HEADS * IDX_DIM, IDX_DIM, IDX_HEADS, MIX_WIDTH)
_OFFS = [0] + [int(v) for v in np.cumsum(SPLIT_SIZES)]
(O_CQ, O_CKV, O_KR, O_QB, O_KB, O_VB, O_QC, O_KC, O_VC, O_QI, O_KI, O_WI, O_GATE, _O_END) = _OFFS

LANES = 128
SUBLANES = 8
LOG2E = 1.4426950408889634
MASKED = -1e30
VMEM_LIMIT = 48 * 1024 * 1024

PROJ_ROWS = 512
MLA_TQ = 512
DIL_N = 128
DIL_TQ = 512
DSA_Q = 256
DSA_ONES = 16
DSA_CH = 512
FLT_MAX = float(np.finfo(np.float32).max)
_KEY_NEG_MAX = int(np.int32(np.uint32(0xFF7FFFFF) ^ np.uint32(0x7FFFFFFF)))
_KEY_FIRST_NAN = 0x7F800001


def _dot(a, b):
    return jnp.dot(a, b, preferred_element_type=F32)


def _dot_nt(a, b):
    return lax.dot_general(a, b, (((1,), (1,)), ((), ())), preferred_element_type=F32)


def _cparams(sem):
    return pltpu.CompilerParams(dimension_semantics=sem, vmem_limit_bytes=VMEM_LIMIT)


def _mod_kernel(c_ref, w_ref, b_ref, o_ref):
    c = c_ref[...]
    sc = (c * jax.nn.sigmoid(c)).astype(BF16)
    o_ref[0] = _dot(sc, w_ref[0].astype(BF16)) + b_ref[0]


def _mod_call(c, w_ada, b_ada):
    depth, d, n3 = w_ada.shape
    bsz = c.shape[0]
    tn = 1024
    return pl.pallas_call(
        _mod_kernel,
        grid=(depth, n3 // tn),
        in_specs=[pl.BlockSpec((bsz, d), lambda l, j: (0, 0)),
                  pl.BlockSpec((1, d, tn), lambda l, j: (l, 0, j)),
                  pl.BlockSpec((1, 1, tn), lambda l, j: (l, 0, j))],
        out_specs=pl.BlockSpec((1, bsz, tn), lambda l, j: (l, 0, j)),
        out_shape=jax.ShapeDtypeStruct((depth, bsz, n3), F32),
        compiler_params=_cparams(("arbitrary", "arbitrary")),
    )(c, w_ada, b_ada.reshape(depth, 1, n3))


def _modulate(x_ref, mod_ref):
    mod = mod_ref[...]
    shift = mod[:, :D_MODEL]
    scale = mod[:, D_MODEL:2 * D_MODEL]
    return (x_ref[...] * (1.0 + scale) + shift).astype(BF16)


def _rms(x, g, eps=1e-6):
    ms = jnp.mean(x * x, axis=-1, keepdims=True)
    return (x * lax.rsqrt(ms + eps) * g).astype(BF16)


def _proj_mla_kernel(x_ref, mod_ref, w1_ref, gq_ref, gkv_ref, wq1_ref, wq2_ref, wk_ref, wv_ref,
                     ct_ref, st_ref, q_ref, k_ref, v_ref):
    h = _modulate(x_ref, mod_ref)
    pa = _dot(h, w1_ref[...])
    cq = _rms(pa[:, :Q_LORA], gq_ref[...])
    ckv = _rms(pa[:, Q_LORA:Q_LORA + KV_LORA], gkv_ref[...])
    ct = ct_ref[...]
    st = st_ref[...]
    ct6 = jnp.concatenate([ct] * A_HEADS, axis=1)
    st6 = jnp.concatenate([st] * A_HEADS, axis=1)
    qscale = (A_NOPE + A_ROPE) ** -0.5 * LOG2E
    q = (_dot(cq, wq1_ref[...]) * ct6 + _dot(cq, wq2_ref[...]) * st6) * qscale
    q_ref[...] = q.astype(BF16)
    o = Q_LORA + KV_LORA
    kr = pa[:, o:o + LANES] * ct + pa[:, o + LANES:o + 2 * LANES] * st
    k = _dot(ckv, wk_ref[...]) + jnp.concatenate([kr] * A_HEADS, axis=1)
    k_ref[...] = k.astype(BF16)
    v_ref[...] = _dot(ckv, wv_ref[...]).astype(BF16)


def _proj_mla_call(x, mod, w1, gq, gkv, wq1, wq2, wk, wv, ct, st):
    bsz, s_len, d = x.shape
    tm = PROJ_ROWS
    hw = A_HEADS * LANES
    full = lambda a: pl.BlockSpec(a.shape, lambda b, i: (0,) * a.ndim)
    return pl.pallas_call(
        _proj_mla_kernel,
        grid=(bsz, s_len // tm),
        in_specs=[pl.BlockSpec((None, tm, d), lambda b, i: (b, i, 0)),
                  pl.BlockSpec((None, 1, 3 * d), lambda b, i: (b, 0, 0)),
                  full(w1), full(gq), full(gkv), full(wq1), full(wq2), full(wk), full(wv),
                  pl.BlockSpec((tm, LANES), lambda b, i: (i, 0)),
                  pl.BlockSpec((tm, LANES), lambda b, i: (i, 0))],
        out_specs=[pl.BlockSpec((None, tm, hw), lambda b, i: (b, i, 0)),
                   pl.BlockSpec((None, tm, hw), lambda b, i: (b, i, 0)),
                   pl.BlockSpec((None, tm, A_WIDTH), lambda b, i: (b, i, 0))],
        out_shape=[jax.ShapeDtypeStruct((bsz, s_len, hw), BF16),
                   jax.ShapeDtypeStruct((bsz, s_len, hw), BF16),
                   jax.ShapeDtypeStruct((bsz, s_len, A_WIDTH), BF16)],
        compiler_params=_cparams(("parallel", "parallel")),
    )(x, mod, w1, gq, gkv, wq1, wq2, wk, wv, ct, st)


_R_QB, _R_KB, _R_VB = 0, B_WIDTH, 2 * B_WIDTH
_R_QC = 3 * B_WIDTH
_R_KC, _R_VC = _R_QC + C_WIDTH, _R_QC + 2 * C_WIDTH
_R_QI = _R_QC + 3 * C_WIDTH
_R_KI = _R_QI + IDX_HEADS * IDX_DIM
_R_WI = _R_KI + 2 * LANES
_R_GATE = _R_WI + LANES
_R_END = _R_GATE + MIX_WIDTH


def _proj_rest_kernel(x_ref, mod_ref, w2_ref, wvt_ref, qb_ref, kb_ref, vb_ref, qc_ref, kc_ref, vct_ref,
                      qi_ref, ki_ref, wi_ref, gate_ref):
    h = _modulate(x_ref, mod_ref)

    def proj(start, width):
        return _dot(h, w2_ref[:, start:start + width])

    inv_sqrt_dh = HEAD_DIM ** -0.5 * LOG2E
    qb_ref[...] = (proj(_R_QB, B_WIDTH) * inv_sqrt_dh).astype(BF16)
    kb_ref[...] = proj(_R_KB, B_WIDTH).astype(BF16)
    vb_ref[...] = proj(_R_VB, B_WIDTH).astype(BF16)
    qc_ref[...] = (proj(_R_QC, C_WIDTH) * inv_sqrt_dh).astype(BF16)
    kc_ref[...] = proj(_R_KC, C_WIDTH).astype(BF16)
    vct_ref[...] = _dot_nt(wvt_ref[...], h).astype(BF16)
    qi_ref[...] = (proj(_R_QI, IDX_HEADS * IDX_DIM) * IDX_DIM ** -0.5).astype(BF16)
    ki_ref[...] = proj(_R_KI, 2 * LANES).astype(BF16)
    wi_ref[...] = proj(_R_WI, LANES) * IDX_HEADS ** -0.5
    gate_ref[...] = proj(_R_GATE, MIX_WIDTH)


def _proj_rest_call(x, mod, w2, wv_t):
    bsz, s_len, d = x.shape
    tm = PROJ_ROWS
    widths = (B_WIDTH, B_WIDTH, B_WIDTH, C_WIDTH, C_WIDTH, C_WIDTH, IDX_HEADS * IDX_DIM,
              2 * LANES, LANES, MIX_WIDTH)
    dtypes = (BF16,) * 8 + (F32, F32)
    return pl.pallas_call(
        _proj_rest_kernel,
        grid=(bsz, s_len // tm),
        in_specs=[pl.BlockSpec((None, tm, d), lambda b, i: (b, i, 0)),
                  pl.BlockSpec((None, 1, 3 * d), lambda b, i: (b, 0, 0)),
                  pl.BlockSpec(w2.shape, lambda b, i: (0, 0)),
                  pl.BlockSpec(wv_t.shape, lambda b, i: (0, 0))],
        out_specs=[pl.BlockSpec((None, C_WIDTH, tm), lambda b, i: (b, 0, i)) if n == 5 else
                   pl.BlockSpec((None, tm, w), lambda b, i: (b, i, 0)) for n, w in enumerate(widths)],
        out_shape=[jax.ShapeDtypeStruct((bsz, C_WIDTH, s_len) if n == 5 else (bsz, s_len, w), dt)
                   for n, (w, dt) in enumerate(zip(widths, dtypes))],
        compiler_params=_cparams(("parallel", "parallel")),
    )(x, mod, w2, wv_t)


def _with_ones(v):
    return jnp.concatenate([v, jnp.ones_like(v)], axis=1)


def _init_state(m_ref, acc_ref):
    m_ref[...] = jnp.full(m_ref.shape, MASKED, F32)
    acc_ref[...] = jnp.zeros(acc_ref.shape, F32)


def _pair_output(acc_ref, i0, i1):
    lane = lax.broadcasted_iota(I32, (acc_ref.shape[1], LANES), 1)
    a0, a1 = acc_ref[i0], acc_ref[i1]
    o0 = a0[:, :LANES] / a0[:, LANES:]
    o1 = a1[:, :LANES] / a1[:, LANES:]
    return jnp.where(lane < HEAD_DIM, o0, o1)


def _mla_kernel(q_ref, k_ref, v_ref, o_ref, m_ref, acc_ref):
    qi = pl.program_id(2)
    tq = q_ref.shape[0]
    _init_state(m_ref, acc_ref)
    lane = lax.broadcasted_iota(I32, (tq, 2 * LANES), 1)
    row = lax.broadcasted_iota(I32, (tq, tq), 0)
    col = lax.broadcasted_iota(I32, (tq, tq), 1)

    def step(c, diagonal):
        ks = pl.ds(pl.multiple_of(c * tq, tq), tq)
        k = k_ref[ks, :]
        zero = jnp.zeros_like(k)
        k2 = jnp.concatenate([jnp.where(lane < LANES, k, zero), jnp.where(lane >= LANES, k, zero)], axis=0)
        s2 = _dot_nt(q_ref[...], k2)
        ps, alphas = [], []
        for hh in range(2):
            s = s2[:, hh * tq:(hh + 1) * tq]
            if diagonal:
                s = jnp.where(col <= row, s, MASKED)
            m_prev = m_ref[hh]
            m_new = jnp.maximum(m_prev, jnp.max(s, axis=1, keepdims=True))
            alphas.append(jnp.exp2(m_prev - m_new))
            ps.append(jnp.exp2(s - jnp.tile(m_new, (1, tq // LANES))).astype(BF16))
            m_ref[hh] = m_new
        pv = _dot(jnp.concatenate(ps, axis=0), _with_ones(v_ref[ks, :]))
        for hh in range(2):
            acc_ref[hh] = jnp.tile(alphas[hh], (1, 2)) * acc_ref[hh] + pv[hh * tq:(hh + 1) * tq]

    def body(c, carry):
        step(c, False)
        return carry

    lax.fori_loop(0, qi, body, 0)
    step(qi, True)
    o_ref[...] = _pair_output(acc_ref, 0, 1)


def _mla_call(q, k, v):
    bsz, s_len, _ = q.shape
    tq = MLA_TQ
    npair = A_HEADS // 2
    return pl.pallas_call(
        _mla_kernel,
        grid=(bsz, npair, s_len // tq),
        in_specs=[pl.BlockSpec((None, tq, 2 * LANES), lambda b, p, i: (b, i, p)),
                  pl.BlockSpec((None, s_len, 2 * LANES), lambda b, p, i: (b, 0, p)),
                  pl.BlockSpec((None, s_len, LANES), lambda b, p, i: (b, 0, p))],
        out_specs=pl.BlockSpec((None, tq, LANES), lambda b, p, i: (b, i, p)),
        out_shape=jax.ShapeDtypeStruct((bsz, s_len, A_WIDTH), F32),
        scratch_shapes=[pltpu.VMEM((2, tq, LANES), F32), pltpu.VMEM((2, tq, 2 * LANES), F32)],
        compiler_params=_cparams(("parallel", "parallel", "arbitrary")),
    )(q, k, v)


def _dil_kernel(q_ref, k_ref, v_ref, bias_ref, o_ref, lse_ref):
    i = pl.program_id(2)
    n = DIL_N
    blocks = q_ref.shape[0] // n
    lane = lax.broadcasted_iota(I32, (2 * n, LANES), 1)
    first = lax.broadcasted_iota(I32, (n, LANES), 1) < HEAD_DIM
    for blk in range(blocks):
        j = i * blocks + blk
        start = pl.multiple_of(jnp.maximum(j - 1, 0) * n, n)
        table = jnp.minimum(j, 1)
        for p in range(B_HEADS // 2):
            cols = slice(p * LANES, (p + 1) * LANES)
            q = q_ref[blk * n:(blk + 1) * n, cols]
            k = k_ref[pl.ds(start, 2 * n), cols]
            v = _with_ones(v_ref[pl.ds(start, 2 * n), cols])
            outs, lses = [], []
            for hh in range(2):
                h = 2 * p + hh
                own = (lane >= hh * HEAD_DIM) & (lane < (hh + 1) * HEAD_DIM)
                s = _dot_nt(q, jnp.where(own, k, jnp.zeros_like(k))) + bias_ref[table, h]
                m = jnp.max(s, axis=1, keepdims=True)
                oe = _dot(jnp.exp2(s - m).astype(BF16), v)
                den = oe[:, LANES:]
                outs.append(oe[:, :LANES] / den)
                lses.append(m + jnp.log2(den))
            o_ref[blk * n:(blk + 1) * n, cols] = jnp.where(first, outs[0], outs[1])
            lse_ref[blk * n:(blk + 1) * n, cols] = jnp.where(first, lses[0], lses[1])


def _dil_bias(dil, slopes_b):
    n = DIL_N
    kj = jnp.arange(2 * n)
    step = jnp.arange(n)[:, None] + n - kj[None, :]
    valid = (step >= 0) & (step <= n)
    bias = -(slopes_b * LOG2E)[:, None, None] * (step * dil).astype(F32)[None]
    banded = jnp.where(valid[None], bias, MASKED)
    first = jnp.concatenate([banded[:, :, n:], jnp.full_like(banded[:, :, n:], MASKED)], axis=2)
    return jnp.stack([first, banded])


def _dil_call(q, k, v, dil, slopes_b):
    bsz, s_len, w = q.shape
    L = s_len // dil
    tq = min(DIL_TQ, L)
    view = lambda a: a.reshape(bsz, L, dil * w)
    bias = _dil_bias(dil, slopes_b)
    o, lse = pl.pallas_call(
        _dil_kernel,
        grid=(bsz, dil, L // tq),
        in_specs=[pl.BlockSpec((None, tq, w), lambda b, r, i: (b, i, r)),
                  pl.BlockSpec((None, L, w), lambda b, r, i: (b, 0, r)),
                  pl.BlockSpec((None, L, w), lambda b, r, i: (b, 0, r)),
                  pl.BlockSpec(bias.shape, lambda b, r, i: (0, 0, 0, 0))],
        out_specs=[pl.BlockSpec((None, tq, w), lambda b, r, i: (b, i, r))] * 2,
        out_shape=[jax.ShapeDtypeStruct((bsz, L, dil * w), F32)] * 2,
        compiler_params=_cparams(("parallel", "parallel", "arbitrary")),
    )(view(q), view(k), view(v), bias)
    return o.reshape(bsz, s_len, w), lse.reshape(bsz, s_len, w)


def _sortable(x):
    b = lax.bitcast_convert_type(x, I32)
    return b ^ ((b >> 31) & 0x7FFFFFFF)


def _unsortable(k):
    return lax.bitcast_convert_type(k ^ ((k >> 31) & 0x7FFFFFFF), F32)


def _dsa_kernel(qi_ref, ki_ref, wi_ref, qc_ref, kc_ref, vt_ref, tri_ref, ztab_ref, wtab_ref, slope_ref, o_ref,
                sc_ref, sel_ref, ist_ref, fst_ref, m_ref, acc_ref):
    qb = pl.program_id(1)
    nq, ch = DSA_Q, DSA_CH
    g = ch // SUBLANES
    q0 = qb * nq
    nk = (q0 + nq + ch - 1) // ch
    key_i = lax.broadcasted_iota(I32, (ch, nq), 0)
    qry_i = lax.broadcasted_iota(I32, (ch, nq), 1)
    rel = key_i - qry_i

    def rows(a):
        return jnp.broadcast_to(a[:1], (SUBLANES, nq))

    def groups(a):
        return a.reshape(g, SUBLANES, nq)

    def gsum(a):
        return jnp.sum(jnp.sum(a.reshape(SUBLANES, g // SUBLANES, SUBLANES, nq), axis=1), axis=0)

    def gmax(a):
        return jnp.max(jnp.max(a.reshape(SUBLANES, g // SUBLANES, SUBLANES, nq), axis=1), axis=0)

    def colsum(a):
        return rows(jnp.sum(gsum(groups(a)), axis=0, keepdims=True))

    w_t = wi_ref[...].T
    w_rows = [w_t[h:h + 1, :] for h in range(IDX_HEADS)]

    def score_body(c, carry):
        s1, s2 = carry
        ks = pl.ds(pl.multiple_of(c * ch, ch), ch)
        k_lo = ki_ref[ks, :LANES]
        k_hi = ki_ref[ks, LANES:]
        sc = jnp.zeros((ch, nq), F32)
        for j in range(IDX_HEADS // 2):
            qj = qi_ref[:, j * LANES:(j + 1) * LANES]
            sc = sc + w_rows[2 * j] * jnp.maximum(_dot_nt(k_lo, qj), 0.0)
            sc = sc + w_rows[2 * j + 1] * jnp.maximum(_dot_nt(k_hi, qj), 0.0)
        causal = rel <= q0 - c * ch
        sc_ref[c] = groups(jnp.where(causal, sc, -jnp.inf))
        kept = jnp.where(causal, sc, 0.0)
        return s1 + colsum(kept), s2 + colsum(kept * kept)

    zeros = jnp.zeros((SUBLANES, nq), F32)
    s1, s2 = lax.fori_loop(0, nk, score_body, (zeros, zeros))

    def count(*preds):
        def body(c, accs):
            s = sc_ref[c]
            return tuple(a + gsum(jnp.where(p(s), 1.0, 0.0)) for a, p in zip(accs, preds))
        accs = lax.fori_loop(0, nk, body, (zeros,) * len(preds))
        return tuple(rows(jnp.sum(a, axis=0, keepdims=True)) for a in accs)

    topk = float(TOPK_MAX)
    target = topk + 0.5
    T_KEY, LO, HI = 0, 1, 2
    C_LO, C_HI, REAL_LO, REAL_HI, SIDE, THR, C_THR, DONE, STEP = range(9)
    fst_ref[THR] = jnp.full((SUBLANES, nq), -FLT_MAX, F32)
    fst_ref[C_THR] = zeros

    @pl.when(q0 + 1 > TOPK_MAX)
    def _():
        n = (q0 + 1 + lax.broadcasted_iota(I32, (SUBLANES, nq), 1)).astype(F32)
        mu = s1 / n
        sigma = jnp.sqrt(jnp.maximum(s2 / n - mu * mu, 0.0))
        fst_ref[STEP] = wtab_ref[...] * sigma * 1.5
        c_zero, c_pos = count(lambda s: s >= 0.0, lambda s: s > 0.0)
        zero_tie = (c_pos < topk) & (c_zero >= topk)
        above = c_pos >= topk
        below = c_zero < topk
        key0 = jnp.zeros((SUBLANES, nq), I32)
        lo0 = jnp.where(above, key0, jnp.full((SUBLANES, nq), _KEY_NEG_MAX, I32))
        hi0 = jnp.where(below, key0, jnp.full((SUBLANES, nq), _KEY_FIRST_NAN, I32))
        ist_ref[LO] = lo0
        ist_ref[HI] = hi0
        ist_ref[T_KEY] = jnp.clip(_sortable(mu + ztab_ref[...] * sigma), lo0 + 1, hi0 - 1)
        fst_ref[C_LO] = jnp.where(above, c_zero, n)
        fst_ref[C_HI] = jnp.where(below, c_zero, 0.0)
        fst_ref[REAL_LO] = jnp.where(above, 1.0, 0.0)
        fst_ref[REAL_HI] = jnp.where(below, 1.0, 0.0)
        fst_ref[SIDE] = zeros
        fst_ref[THR] = jnp.where(zero_tie, 0.0, -FLT_MAX)
        fst_ref[C_THR] = jnp.where(zero_tie, c_zero, 0.0)
        done0 = jnp.where(zero_tie, 1.0, 0.0)
        fst_ref[DONE] = done0

        def cond(st):
            return jnp.logical_and(st[0] < 60, st[1] > 0.0)

        def body(st):
            it = st[0]
            t_key, lo, hi = ist_ref[T_KEY], ist_ref[LO], ist_ref[HI]
            c_lo, c_hi, done, side = fst_ref[C_LO], fst_ref[C_HI], fst_ref[DONE], fst_ref[SIDE]
            t = _unsortable(t_key)
            (c,) = count(lambda s: s >= t[None])
            live = done == 0.0
            hit = live & (c == topk)
            up = live & (c > topk)
            dn = live & (c < topk)
            moved = jnp.where(up, 1.0, jnp.where(dn, -1.0, 0.0))
            repeat = (moved == side) & (moved != 0.0)
            lo = jnp.where(up, t_key, lo)
            c_lo = jnp.where(up, c, c_lo)
            real_lo = jnp.where(up, 1.0, fst_ref[REAL_LO])
            hi = jnp.where(dn, t_key, hi)
            c_hi = jnp.where(dn, c, c_hi)
            real_hi = jnp.where(dn, 1.0, fst_ref[REAL_HI])
            gap = lax.shift_right_logical(hi - lo, 1)
            shut = (gap == 0) & live & jnp.logical_not(hit)
            fst_ref[THR] = jnp.where(hit, t, jnp.where(shut, _unsortable(lo), fst_ref[THR]))
            fst_ref[C_THR] = jnp.where(hit, c, jnp.where(shut, c_lo, fst_ref[C_THR]))
            done = jnp.where(hit | shut, 1.0, done)
            lo_f, hi_f = _unsortable(lo), _unsortable(hi)
            both = real_lo * real_hi
            frac = jnp.where(repeat, 0.5, (c_lo - target) / (c_lo - c_hi))
            inside = lo_f + (hi_f - lo_f) * frac
            guess = jnp.where(both > 0.0, inside, t + fst_ref[STEP] * (c - target))
            g_key = jnp.clip(_sortable(guess), lo + 1, hi - 1)
            late = jnp.where(it >= 14, 1.0, 0.0)
            ist_ref[T_KEY] = jnp.where(late > 0.0, lo + gap, g_key)
            ist_ref[LO] = lo
            ist_ref[HI] = hi
            fst_ref[C_LO] = c_lo
            fst_ref[C_HI] = c_hi
            fst_ref[REAL_LO] = real_lo
            fst_ref[REAL_HI] = real_hi
            fst_ref[SIDE] = jnp.where(moved != 0.0, moved, side)
            fst_ref[DONE] = done
            return it + 1, float(nq) - jnp.sum(done[:1])

        lax.while_loop(cond, body, (jnp.int32(0), float(nq) - jnp.sum(done0[:1])))

    thr = fst_ref[THR]
    has_tie = jnp.max(fst_ref[C_THR]) > topk

    @pl.when(jnp.logical_not(has_tie))
    def _():
        def body(c, carry):
            sel_ref[c] = jnp.where(sc_ref[c] >= thr[None], 0.0, MASKED)
            return carry
        lax.fori_loop(0, nk, body, 0)

    @pl.when(has_tie)
    def _():
        (n_gt,) = count(lambda s: s > thr[None])
        need = topk - n_gt

        def body(c, seen):
            s = sc_ref[c]
            eq = s == thr[None]
            ones = jnp.where(eq, 1.0, 0.0).reshape(ch, nq).astype(BF16)
            rank = groups(_dot(tri_ref[...], ones)) + seen[None]
            take = (s > thr[None]) | (eq & (rank <= need[None]))
            sel_ref[c] = jnp.where(take, 0.0, MASKED)
            return rows(rank[g - 1, SUBLANES - 1:SUBLANES, :])
        lax.fori_loop(0, nk, body, zeros)

    m_ref[...] = jnp.full(m_ref.shape, MASKED, F32)
    acc_ref[...] = jnp.zeros(acc_ref.shape, F32)
    lane4 = lax.broadcasted_iota(I32, (ch, C_WIDTH), 1)
    relf = rel.astype(F32)
    ones_rows = jnp.ones((DSA_ONES, ch), BF16)
    ag = acc_ref.shape[1] // SUBLANES

    def attn_body(c, carry):
        ks = pl.ds(pl.multiple_of(c * ch, ch), ch)
        dist = relf + (c * ch - q0).astype(F32)
        sel = sel_ref[c].reshape(ch, nq)
        k = kc_ref[ks, :]
        zero = jnp.zeros_like(k)
        k4 = jnp.concatenate([jnp.where((lane4 >= h * HEAD_DIM) & (lane4 < (h + 1) * HEAD_DIM), k, zero)
                              for h in range(C_HEADS)], axis=0)
        s4 = _dot_nt(k4, qc_ref[...])
        prs, alphas = [], []
        for h in range(C_HEADS):
            s = groups(s4[h * ch:(h + 1) * ch] + (slope_ref[h] * dist + sel))
            m_prev = m_ref[h]
            m_new = jnp.maximum(m_prev, rows(jnp.max(gmax(s), axis=0, keepdims=True)))
            alphas.append(jnp.exp2(m_prev - m_new))
            prs.append(jnp.exp2(s - m_new[None]).reshape(ch, nq).astype(BF16))
            m_ref[h] = m_new
        for p in range(C_HEADS // 2):
            v_t = jnp.concatenate([vt_ref[p * LANES:(p + 1) * LANES, ks], ones_rows], axis=0)
            pv = _dot(v_t, jnp.concatenate(prs[2 * p:2 * p + 2], axis=1))
            for hh in range(2):
                h = 2 * p + hh
                acc = acc_ref[h].reshape(ag, SUBLANES, nq) * alphas[h][None]
                acc_ref[h] = acc.reshape(ag * SUBLANES, nq) + pv[:, hh * nq:(hh + 1) * nq]
        return carry

    lax.fori_loop(0, nk, attn_body, 0)
    dim = lax.broadcasted_iota(I32, (LANES, nq), 0)
    for p in range(C_HEADS // 2):
        outs = []
        for hh in range(2):
            acc = acc_ref[2 * p + hh]
            den = acc[LANES:LANES + SUBLANES]
            outs.append((acc[:LANES].reshape(LANES // SUBLANES, SUBLANES, nq) / den[None]).reshape(LANES, nq))
        o_ref[:, p * LANES:(p + 1) * LANES] = jnp.where(dim < HEAD_DIM, outs[0], outs[1]).T


def _quantile_table(s_len):
    n = jnp.arange(1, s_len + 1, dtype=F32)
    tail = jnp.clip((TOPK_MAX + 0.5) / n, 1e-6, 1.0 - 1e-6)
    z = jax.scipy.special.ndtri(1.0 - tail)
    pdf = jnp.exp(-0.5 * z * z) * (2.0 * np.pi) ** -0.5
    rep = lambda a: jnp.broadcast_to(a[None, :], (SUBLANES, s_len))
    return rep(z), rep(1.0 / (n * pdf))


def _dsa_call(qi, ki, wi, qc, kc, vc_t, slopes_c):
    bsz, s_len, _ = qc.shape
    nq, ch = DSA_Q, DSA_CH
    tri = (jnp.arange(ch)[:, None] >= jnp.arange(ch)[None, :]).astype(BF16)
    ztab, wtab = _quantile_table(s_len)
    blk = lambda w: pl.BlockSpec((None, nq, w), lambda b, i: (b, i, 0))
    seq = lambda w: pl.BlockSpec((None, s_len, w), lambda b, i: (b, 0, 0))
    tab = pl.BlockSpec((SUBLANES, nq), lambda b, i: (0, i))
    return pl.pallas_call(
        _dsa_kernel,
        grid=(bsz, s_len // nq),
        in_specs=[blk(IDX_HEADS * IDX_DIM), seq(2 * LANES), blk(LANES),
                  blk(C_WIDTH), seq(C_WIDTH),
                  pl.BlockSpec((None, C_WIDTH, s_len), lambda b, i: (b, 0, 0)),
                  pl.BlockSpec((ch, ch), lambda b, i: (0, 0)),
                  tab, tab,
                  pl.BlockSpec(memory_space=pltpu.SMEM)],
        out_specs=blk(C_WIDTH),
        out_shape=jax.ShapeDtypeStruct((bsz, s_len, C_WIDTH), F32),
        scratch_shapes=[pltpu.VMEM((s_len // ch, ch // SUBLANES, SUBLANES, nq), F32),
                        pltpu.VMEM((s_len // ch, ch // SUBLANES, SUBLANES, nq), F32),
                        pltpu.VMEM((3, SUBLANES, nq), I32),
                        pltpu.VMEM((9, SUBLANES, nq), F32),
                        pltpu.VMEM((C_HEADS, SUBLANES, nq), F32),
                        pltpu.VMEM((C_HEADS, LANES + DSA_ONES, nq), F32)],
        compiler_params=_cparams(("parallel", "arbitrary")),
    )(qi, ki, wi, qc, kc, vc_t, tri, ztab, wtab, slopes_c * LOG2E)


def _out_kernel(alpha, x_ref, mod_ref, oa_ref, ob1_ref, ob2_ref, ob3_ref, l1_ref, l2_ref, l3_ref,
                oc_ref, gate_ref, wo_ref, g_ref, b_ref, y_ref):
    l1, l2, l3 = l1_ref[...], l2_ref[...], l3_ref[...]
    mx = jnp.maximum(jnp.maximum(l1, l2), l3)
    e1, e2, e3 = jnp.exp2(l1 - mx), jnp.exp2(l2 - mx), jnp.exp2(l3 - mx)
    ob = (e1 * ob1_ref[...] + e2 * ob2_ref[...] + e3 * ob3_ref[...]) / (e1 + e2 + e3)
    pg = gate_ref[...]
    sg = pg * jax.nn.sigmoid(pg)
    ya = (oa_ref[...] * sg[:, :A_WIDTH]).astype(BF16)
    yb = (ob * sg[:, A_WIDTH:A_WIDTH + B_WIDTH]).astype(BF16)
    yc = (oc_ref[...] * sg[:, A_WIDTH + B_WIDTH:]).astype(BF16)
    sub = (_dot(ya, wo_ref[:A_WIDTH, :]) + _dot(yb, wo_ref[A_WIDTH:A_WIDTH + B_WIDTH, :])
           + _dot(yc, wo_ref[A_WIDTH + B_WIDTH:, :]))
    gate = mod_ref[...][:, 2 * D_MODEL:]
    z = alpha * x_ref[...] + (1.0 + gate) * sub
    mu = jnp.mean(z, axis=-1, keepdims=True)
    zc = z - mu
    var = jnp.mean(zc * zc, axis=-1, keepdims=True)
    y_ref[...] = zc * lax.rsqrt(var + 1e-5) * g_ref[...] + b_ref[...]


def _out_call(alpha, x, mod, oa, obs, lses, oc, gate, wo, g, b):
    bsz, s_len, d = x.shape
    tm = PROJ_ROWS
    rows = lambda w: pl.BlockSpec((None, tm, w), lambda bb, i: (bb, i, 0))
    full = lambda a: pl.BlockSpec(a.shape, lambda bb, i: (0,) * a.ndim)
    return pl.pallas_call(
        functools.partial(_out_kernel, alpha),
        grid=(bsz, s_len // tm),
        in_specs=[rows(d), pl.BlockSpec((None, 1, 3 * d), lambda bb, i: (bb, 0, 0)),
                  rows(A_WIDTH)] + [rows(B_WIDTH)] * 6 + [rows(C_WIDTH), rows(MIX_WIDTH),
                  full(wo), full(g), full(b)],
        out_specs=rows(d),
        out_shape=jax.ShapeDtypeStruct((bsz, s_len, d), F32),
        compiler_params=_cparams(("parallel", "parallel")),
    )(x, mod, oa, *obs, *lses, oc, gate, wo, g, b)


def _pad_cols(a, width):
    return jnp.pad(a, ((0, 0), (0, width - a.shape[1])))


def _swap_halves(a):
    half = a.shape[1] // 2
    return jnp.concatenate([a[:, half:], a[:, :half]], axis=1)


def _layer_weights(w_in, w_uq, w_uk, w_uv, w_out):
    d = w_in.shape[0]
    z = lambda n: jnp.zeros((d, n), w_in.dtype)
    w_kr = w_in[:, O_KR:O_QB]
    w1 = jnp.concatenate([w_in[:, O_CQ:O_KR],
                          z(A_NOPE), w_kr, z(LANES - A_NOPE - A_ROPE),
                          z(A_NOPE), _swap_halves(w_kr), z(LANES - A_NOPE - A_ROPE)], axis=1)
    hq = A_NOPE + A_ROPE
    q1, q2 = [], []
    for h in range(A_HEADS):
        blk = w_uq[:, h * hq:(h + 1) * hq]
        q1.append(_pad_cols(blk, LANES))
        zq = jnp.zeros((w_uq.shape[0], A_NOPE), w_uq.dtype)
        q2.append(_pad_cols(jnp.concatenate([zq, _swap_halves(blk[:, A_NOPE:])], axis=1), LANES))
    wq1 = jnp.concatenate(q1, axis=1)
    wq2 = jnp.concatenate(q2, axis=1)
    wk = jnp.concatenate([_pad_cols(w_uk[:, h * A_NOPE:(h + 1) * A_NOPE], LANES)
                          for h in range(A_HEADS)], axis=1)
    w_ki = w_in[:, O_KI:O_WI]
    w2 = jnp.concatenate([w_in[:, O_QB:O_KI],
                          w_ki, z(LANES - IDX_DIM), z(LANES - IDX_DIM), w_ki,
                          w_in[:, O_WI:O_GATE], z(LANES - IDX_HEADS),
                          w_in[:, O_GATE:]], axis=1)
    bf = lambda a: a.astype(BF16)
    wv_t = w_in[:, O_VC:O_QI].T
    return bf(w1), bf(wq1), bf(wq2), bf(wk), bf(w_uv), bf(w2), bf(wv_t), bf(w_out)


def _rope_tables(s_len):
    pos = jnp.arange(s_len, dtype=F32)
    freqs = ROPE_THETA ** (-jnp.arange(0, A_ROPE, 2, dtype=F32) / A_ROPE)
    ang = pos[:, None] * freqs[None, :]
    cos, sin = jnp.cos(ang), jnp.sin(ang)
    ones = jnp.ones((s_len, A_NOPE), F32)
    zeros = jnp.zeros((s_len, LANES - A_NOPE - A_ROPE), F32)
    ct = jnp.concatenate([ones, cos, cos, zeros], axis=1)
    st = jnp.concatenate([jnp.zeros((s_len, A_NOPE), F32), -sin, sin, zeros], axis=1)
    return ct, st


def kernel(x, c, w_ada, b_ada, w_in, q_norm_g, kv_norm_g, w_uq, w_uk, w_uv, w_out, ln_g, ln_b):
    bsz, s_len, d = x.shape
    depth = w_ada.shape[0]
    alpha = (2 * depth) ** 0.25
    slopes = 2.0 ** (-8.0 * jnp.arange(1, N_ALIBI + 1, dtype=F32) / N_ALIBI)
    slopes_b, slopes_c = slopes[:B_HEADS], slopes[B_HEADS:]
    ct, st = _rope_tables(s_len)
    mod_all = _mod_call(c, w_ada, b_ada)

    for l in range(depth):
        w1, wq1, wq2, wk, wv, w2, wv_t, wo = _layer_weights(w_in[l], w_uq[l], w_uk[l], w_uv[l], w_out[l])
        mod = mod_all[l].reshape(bsz, 1, 3 * d)
        qa, ka, va = _proj_mla_call(x, mod, w1, q_norm_g[l].reshape(1, -1), kv_norm_g[l].reshape(1, -1),
                                    wq1, wq2, wk, wv, ct, st)
        qb, kb, vb, qc, kc, vc_t, qi, ki, wi, gate = _proj_rest_call(x, mod, w2, wv_t)
        o_a = _mla_call(qa, ka, va)
        obs, lses = [], []
        for window, dil in DILATED_PATTERNS:
            assert window // dil == DIL_N
            o, lse = _dil_call(qb, kb, vb, dil, slopes_b)
            obs.append(o)
            lses.append(lse)
        o_c = _dsa_call(qi, ki, wi, qc, kc, vc_t, slopes_c)
        x = _out_call(alpha, x, mod, o_a, obs, lses, o_c, gate, wo,
                      ln_g[l].reshape(1, -1), ln_b[l].reshape(1, -1))
    return x
```

```python
import functools

import numpy as np
import jax
import jax.numpy as jnp
from jax import lax
from jax.experimental import pallas as pl
from jax.experimental.pallas import tpu as pltpu

F32 = jnp.float32
BF16 = jnp.bfloat16
I32 = jnp.int32

D_MODEL = 1024
HEAD_DIM = 64
A_HEADS = 6
A_NOPE = 64
A_ROPE = 32
A_V = 64
Q_LORA = 768
KV_LORA = 256
ROPE_THETA = 10000.0
B_HEADS = 6
DILATED_PATTERNS = ((128, 1), (512, 4), (2048, 16))
C_HEADS = 4
IDX_HEADS = 8
IDX_DIM = 64
TOPK_MAX = 256
A_WIDTH = A_HEADS * A_V
B_WIDTH = B_HEADS * HEAD_DIM
C_WIDTH = C_HEADS * HEAD_DIM
MIX_WIDTH = A_WIDTH + B_WIDTH + C_WIDTH
N_ALIBI = B_HEADS + C_HEADS
SPLIT_SIZES = (Q_LORA, KV_LORA, A_ROPE, B_WIDTH, B_WIDTH, B_WIDTH, C_WIDTH, C_WIDTH, C_WIDTH,
               IDX_HEADS * IDX_DIM, IDX_DIM, IDX_HEADS, MIX_WIDTH)
_OFFS = [0] + [int(v) for v in np.cumsum(SPLIT_SIZES)]
(O_CQ, O_CKV, O_KR, O_QB, O_KB, O_VB, O_QC, O_KC, O_VC, O_QI, O_KI, O_WI, O_GATE, _O_END) = _OFFS

LANES = 128
SUBLANES = 8
LOG2E = 1.4426950408889634
MASKED = -1e30
VMEM_LIMIT = 48 * 1024 * 1024

PROJ_ROWS = 512
MLA_TQ = 512
DIL_N = 128
DIL_UNROLL = 8
DIL_MIX_ROWS = 512
DSA_Q = 256
DSA_ONES = 16
DSA_CH = 512
FLT_MAX = float(np.finfo(np.float32).max)
_KEY_NEG_MAX = int(np.int32(np.uint32(0xFF7FFFFF) ^ np.uint32(0x7FFFFFFF)))
_KEY_FIRST_NAN = 0x7F800001


def _dot(a, b):
    return jnp.dot(a, b, preferred_element_type=F32)


def _dot_nt(a, b):
    return lax.dot_general(a, b, (((1,), (1,)), ((), ())), preferred_element_type=F32)


def _cparams(sem):
    return pltpu.CompilerParams(dimension_semantics=sem, vmem_limit_bytes=VMEM_LIMIT)


def _mod_kernel(c_ref, w_ref, b_ref, o_ref):
    c = c_ref[...]
    sc = (c * jax.nn.sigmoid(c)).astype(BF16)
    o_ref[0] = _dot(sc, w_ref[0].astype(BF16)) + b_ref[0]


def _mod_call(c, w_ada, b_ada):
    depth, d, n3 = w_ada.shape
    bsz = c.shape[0]
    tn = 1024
    return pl.pallas_call(
        _mod_kernel,
        grid=(depth, n3 // tn),
        in_specs=[pl.BlockSpec((bsz, d), lambda l, j: (0, 0)),
                  pl.BlockSpec((1, d, tn), lambda l, j: (l, 0, j)),
                  pl.BlockSpec((1, 1, tn), lambda l, j: (l, 0, j))],
        out_specs=pl.BlockSpec((1, bsz, tn), lambda l, j: (l, 0, j)),
        out_shape=jax.ShapeDtypeStruct((depth, bsz, n3), F32),
        compiler_params=_cparams(("arbitrary", "arbitrary")),
    )(c, w_ada, b_ada.reshape(depth, 1, n3))


def _modulate(x_ref, mod_ref):
    mod = mod_ref[...]
    shift = mod[:, :D_MODEL]
    scale = mod[:, D_MODEL:2 * D_MODEL]
    return (x_ref[...] * (1.0 + scale) + shift).astype(BF16)


def _rms(x, g, eps=1e-6):
    ms = jnp.mean(x * x, axis=-1, keepdims=True)
    return (x * lax.rsqrt(ms + eps) * g).astype(BF16)


def _proj_mla_kernel(x_ref, mod_ref, w1_ref, gq_ref, gkv_ref, wq1_ref, wq2_ref, wk_ref, wv_ref,
                     ct_ref, st_ref, q_ref, k_ref, v_ref):
    h = _modulate(x_ref, mod_ref)
    pa = _dot(h, w1_ref[...])
    cq = _rms(pa[:, :Q_LORA], gq_ref[...])
    ckv = _rms(pa[:, Q_LORA:Q_LORA + KV_LORA], gkv_ref[...])
    ct = ct_ref[...]
    st = st_ref[...]
    ct6 = jnp.concatenate([ct] * A_HEADS, axis=1)
    st6 = jnp.concatenate([st] * A_HEADS, axis=1)
    qscale = (A_NOPE + A_ROPE) ** -0.5 * LOG2E
    q = (_dot(cq, wq1_ref[...]) * ct6 + _dot(cq, wq2_ref[...]) * st6) * qscale
    q_ref[...] = q.astype(BF16)
    o = Q_LORA + KV_LORA
    kr = pa[:, o:o + LANES] * ct + pa[:, o + LANES:o + 2 * LANES] * st
    k = _dot(ckv, wk_ref[...]) + jnp.concatenate([kr] * A_HEADS, axis=1)
    k_ref[...] = k.astype(BF16)
    v_ref[...] = _dot(ckv, wv_ref[...]).astype(BF16)


def _proj_mla_call(x, mod, w1, gq, gkv, wq1, wq2, wk, wv, ct, st):
    bsz, s_len, d = x.shape
    tm = PROJ_ROWS
    hw = A_HEADS * LANES
    full = lambda a: pl.BlockSpec(a.shape, lambda b, i: (0,) * a.ndim)
    return pl.pallas_call(
        _proj_mla_kernel,
        grid=(bsz, s_len // tm),
        in_specs=[pl.BlockSpec((None, tm, d), lambda b, i: (b, i, 0)),
                  pl.BlockSpec((None, 1, 3 * d), lambda b, i: (b, 0, 0)),
                  full(w1), full(gq), full(gkv), full(wq1), full(wq2), full(wk), full(wv),
                  pl.BlockSpec((tm, LANES), lambda b, i: (i, 0)),
                  pl.BlockSpec((tm, LANES), lambda b, i: (i, 0))],
        out_specs=[pl.BlockSpec((None, tm, hw), lambda b, i: (b, i, 0)),
                   pl.BlockSpec((None, tm, hw), lambda b, i: (b, i, 0)),
                   pl.BlockSpec((None, tm, A_WIDTH), lambda b, i: (b, i, 0))],
        out_shape=[jax.ShapeDtypeStruct((bsz, s_len, hw), BF16),
                   jax.ShapeDtypeStruct((bsz, s_len, hw), BF16),
                   jax.ShapeDtypeStruct((bsz, s_len, A_WIDTH), BF16)],
        compiler_params=_cparams(("parallel", "parallel")),
    )(x, mod, w1, gq, gkv, wq1, wq2, wk, wv, ct, st)


_R_QB, _R_KB, _R_VB = 0, B_WIDTH, 2 * B_WIDTH
_R_QC = 3 * B_WIDTH
_R_KC, _R_VC = _R_QC + C_WIDTH, _R_QC + 2 * C_WIDTH
_R_QI = _R_QC + 3 * C_WIDTH
_R_KI = _R_QI + IDX_HEADS * IDX_DIM
_R_WI = _R_KI + 2 * LANES
_R_GATE = _R_WI + LANES
_R_END = _R_GATE + MIX_WIDTH


def _proj_rest_kernel(x_ref, mod_ref, w2_ref, wvt_ref, qb_ref, kb_ref, vb_ref, qc_ref, kc_ref, vct_ref,
                      qi_ref, ki_ref, wi_ref, gate_ref):
    h = _modulate(x_ref, mod_ref)

    def proj(start, width):
        return _dot(h, w2_ref[:, start:start + width])

    inv_sqrt_dh = HEAD_DIM ** -0.5 * LOG2E
    for ref, val in ((qb_ref, proj(_R_QB, B_WIDTH) * inv_sqrt_dh), (kb_ref, proj(_R_KB, B_WIDTH)),
                     (vb_ref, proj(_R_VB, B_WIDTH))):
        for p in range(B_HEADS // 2):
            ref[p] = val[:, p * LANES:(p + 1) * LANES]
    qc_ref[...] = (proj(_R_QC, C_WIDTH) * inv_sqrt_dh).astype(BF16)
    kc_ref[...] = proj(_R_KC, C_WIDTH).astype(BF16)
    vct_ref[...] = _dot_nt(wvt_ref[...], h).astype(BF16)
    qi_ref[...] = (proj(_R_QI, IDX_HEADS * IDX_DIM) * IDX_DIM ** -0.5).astype(BF16)
    ki_ref[...] = proj(_R_KI, 2 * LANES).astype(BF16)
    wi_ref[...] = proj(_R_WI, LANES) * IDX_HEADS ** -0.5
    gate_ref[...] = proj(_R_GATE, MIX_WIDTH)


def _proj_rest_call(x, mod, w2, wv_t):
    bsz, s_len, d = x.shape
    tm = PROJ_ROWS
    widths = (B_WIDTH, B_WIDTH, B_WIDTH, C_WIDTH, C_WIDTH, C_WIDTH, IDX_HEADS * IDX_DIM,
              2 * LANES, LANES, MIX_WIDTH)
    dtypes = (F32,) * 3 + (BF16,) * 5 + (F32, F32)
    slab = pl.BlockSpec((None, B_HEADS // 2, tm, LANES), lambda b, i: (b, 0, i, 0))
    slab_shape = (bsz, B_HEADS // 2, s_len, LANES)
    return pl.pallas_call(
        _proj_rest_kernel,
        grid=(bsz, s_len // tm),
        in_specs=[pl.BlockSpec((None, tm, d), lambda b, i: (b, i, 0)),
                  pl.BlockSpec((None, 1, 3 * d), lambda b, i: (b, 0, 0)),
                  pl.BlockSpec(w2.shape, lambda b, i: (0, 0)),
                  pl.BlockSpec(wv_t.shape, lambda b, i: (0, 0))],
        out_specs=[slab if n < 3 else pl.BlockSpec((None, C_WIDTH, tm), lambda b, i: (b, 0, i)) if n == 5 else
                   pl.BlockSpec((None, tm, w), lambda b, i: (b, i, 0)) for n, w in enumerate(widths)],
        out_shape=[jax.ShapeDtypeStruct(slab_shape if n < 3 else (bsz, C_WIDTH, s_len) if n == 5 else
                                        (bsz, s_len, w), dt) for n, (w, dt) in enumerate(zip(widths, dtypes))],
        compiler_params=_cparams(("parallel", "parallel")),
    )(x, mod, w2, wv_t)


def _with_ones(v):
    return jnp.concatenate([v, jnp.ones_like(v)], axis=1)


def _init_state(m_ref, acc_ref):
    m_ref[...] = jnp.full(m_ref.shape, MASKED, F32)
    acc_ref[...] = jnp.zeros(acc_ref.shape, F32)


def _pair_output(acc_ref, i0, i1):
    lane = lax.broadcasted_iota(I32, (acc_ref.shape[1], LANES), 1)
    a0, a1 = acc_ref[i0], acc_ref[i1]
    o0 = a0[:, :LANES] / a0[:, LANES:]
    o1 = a1[:, :LANES] / a1[:, LANES:]
    return jnp.where(lane < HEAD_DIM, o0, o1)


def _mla_kernel(q_ref, k_ref, v_ref, o_ref, m_ref, acc_ref):
    qi = pl.program_id(2)
    tq = q_ref.shape[0]
    _init_state(m_ref, acc_ref)
    lane = lax.broadcasted_iota(I32, (tq, 2 * LANES), 1)
    row = lax.broadcasted_iota(I32, (tq, tq), 0)
    col = lax.broadcasted_iota(I32, (tq, tq), 1)

    def step(c, diagonal):
        ks = pl.ds(pl.multiple_of(c * tq, tq), tq)
        k = k_ref[ks, :]
        zero = jnp.zeros_like(k)
        k2 = jnp.concatenate([jnp.where(lane < LANES, k, zero), jnp.where(lane >= LANES, k, zero)], axis=0)
        s2 = _dot_nt(q_ref[...], k2)
        ps, alphas = [], []
        for hh in range(2):
            s = s2[:, hh * tq:(hh + 1) * tq]
            if diagonal:
                s = jnp.where(col <= row, s, MASKED)
            m_prev = m_ref[hh]
            m_new = jnp.maximum(m_prev, jnp.max(s, axis=1, keepdims=True))
            alphas.append(jnp.exp2(m_prev - m_new))
            ps.append(jnp.exp2(s - jnp.tile(m_new, (1, tq // LANES))).astype(BF16))
            m_ref[hh] = m_new
        pv = _dot(jnp.concatenate(ps, axis=0), _with_ones(v_ref[ks, :]))
        for hh in range(2):
            acc_ref[hh] = jnp.tile(alphas[hh], (1, 2)) * acc_ref[hh] + pv[hh * tq:(hh + 1) * tq]

    def body(c, carry):
        step(c, False)
        return carry

    lax.fori_loop(0, qi, body, 0)
    step(qi, True)
    o_ref[...] = _pair_output(acc_ref, 0, 1)


def _mla_call(q, k, v):
    bsz, s_len, _ = q.shape
    tq = MLA_TQ
    npair = A_HEADS // 2
    return pl.pallas_call(
        _mla_kernel,
        grid=(bsz, npair, s_len // tq),
        in_specs=[pl.BlockSpec((None, tq, 2 * LANES), lambda b, p, i: (b, i, p)),
                  pl.BlockSpec((None, s_len, 2 * LANES), lambda b, p, i: (b, 0, p)),
                  pl.BlockSpec((None, s_len, LANES), lambda b, p, i: (b, 0, p))],
        out_specs=pl.BlockSpec((None, tq, LANES), lambda b, p, i: (b, i, p)),
        out_shape=jax.ShapeDtypeStruct((bsz, s_len, A_WIDTH), F32),
        scratch_shapes=[pltpu.VMEM((2, tq, LANES), F32), pltpu.VMEM((2, tq, 2 * LANES), F32)],
        compiler_params=_cparams(("parallel", "parallel", "arbitrary")),
    )(q, k, v)


def _dil_kernel(q_ref, k_ref, v_ref, bias_ref, o_ref, oacc_ref, lse_ref):
    n = DIL_N
    s_len = q_ref.shape[0]
    lane = lax.broadcasted_iota(I32, (2 * n, LANES), 1)
    first = lax.broadcasted_iota(I32, (n, LANES), 1) < HEAD_DIM

    for pat, (_, dil) in enumerate(DILATED_PATTERNS):
        nb = s_len // dil // n
        stride = dil if dil > 1 else None

        def unit(t, pat=pat, dil=dil, nb=nb, stride=stride):
            r, j = t // nb, t % nb
            q_start = r + j * (n * dil)
            b_start = r + jnp.maximum(j - 1, 0) * (n * dil)
            table = jnp.minimum(j, 1)
            q = q_ref[pl.ds(q_start, n, stride=stride), :].astype(BF16)
            k = k_ref[pl.ds(b_start, 2 * n, stride=stride), :].astype(BF16)
            v = _with_ones(v_ref[pl.ds(b_start, 2 * n, stride=stride), :].astype(BF16))
            zero = jnp.zeros_like(k)
            k2 = jnp.concatenate([jnp.where(lane < HEAD_DIM, k, zero), jnp.where(lane >= HEAD_DIM, k, zero)],
                                 axis=0)
            s2 = _dot_nt(q, k2)
            ps, ms = [], []
            for hh in range(2):
                s = s2[:, hh * 2 * n:(hh + 1) * 2 * n] + bias_ref[pat, table, hh]
                m = jnp.max(s, axis=1, keepdims=True)
                ps.append(jnp.exp2(s - m).astype(BF16))
                ms.append(m)
            pv = _dot(jnp.concatenate(ps, axis=0), v)
            outs = [pv[hh * n:(hh + 1) * n, :LANES] / pv[hh * n:(hh + 1) * n, LANES:] for hh in range(2)]
            lses = [ms[hh] + jnp.log2(pv[hh * n:(hh + 1) * n, LANES:]) for hh in range(2)]
            oacc_ref[pat, pl.ds(q_start, n, stride=stride), :] = jnp.where(first, outs[0], outs[1])
            lse_ref[pat, pl.ds(q_start, n, stride=stride), :] = jnp.where(first, lses[0], lses[1])

        def body(i, carry, unit=unit):
            for u in range(DIL_UNROLL):
                unit(i * DIL_UNROLL + u)
            return carry

        lax.fori_loop(0, s_len // n // DIL_UNROLL, body, 0)

    def mix(i, carry):
        rs = pl.ds(pl.multiple_of(i * DIL_MIX_ROWS, DIL_MIX_ROWS), DIL_MIX_ROWS)
        ls = [lse_ref[p, rs, :] for p in range(len(DILATED_PATTERNS))]
        mx = functools.reduce(jnp.maximum, ls)
        es = [jnp.exp2(l - mx) for l in ls]
        num = functools.reduce(lambda a, b: a + b, [e * oacc_ref[p, rs, :] for p, e in enumerate(es)])
        o_ref[rs, :] = num / functools.reduce(lambda a, b: a + b, es)
        return carry

    lax.fori_loop(0, s_len // DIL_MIX_ROWS, mix, 0)


def _dil_bias(slopes_b):
    n = DIL_N
    kj = jnp.arange(2 * n)
    step = jnp.arange(n)[:, None] + n - kj[None, :]
    valid = (step >= 0) & (step <= n)
    tables = []
    for _, dil in DILATED_PATTERNS:
        bias = -(slopes_b * LOG2E)[:, None, None] * (step * dil).astype(F32)[None]
        banded = jnp.where(valid[None], bias, MASKED)
        first = jnp.concatenate([banded[:, :, n:], jnp.full_like(banded[:, :, n:], MASKED)], axis=2)
        tables.append(jnp.stack([first, banded]))
    return jnp.stack(tables).reshape(len(DILATED_PATTERNS), 2, B_HEADS // 2, 2, n, 2 * n)


def _dil_call(q, k, v, slopes_b):
    bsz, npair, s_len, _ = q.shape
    npat = len(DILATED_PATTERNS)
    for window, dil in DILATED_PATTERNS:
        assert window // dil == DIL_N and s_len % (dil * DIL_N * 2) == 0
    bias = _dil_bias(slopes_b)
    slab = pl.BlockSpec((None, None, s_len, LANES), lambda b, p: (b, p, 0, 0))
    return pl.pallas_call(
        _dil_kernel,
        grid=(bsz, npair),
        in_specs=[slab, slab, slab,
                  pl.BlockSpec((npat, 2, None, 2, DIL_N, 2 * DIL_N), lambda b, p: (0, 0, p, 0, 0, 0))],
        out_specs=pl.BlockSpec((None, s_len, LANES), lambda b, p: (b, 0, p)),
        out_shape=jax.ShapeDtypeStruct((bsz, s_len, npair * LANES), F32),
        scratch_shapes=[pltpu.VMEM((npat, s_len, LANES), F32), pltpu.VMEM((npat, s_len, LANES), F32)],
        compiler_params=_cparams(("parallel", "parallel")),
    )(q, k, v, bias)


def _sortable(x):
    b = lax.bitcast_convert_type(x, I32)
    return b ^ ((b >> 31) & 0x7FFFFFFF)


def _unsortable(k):
    return lax.bitcast_convert_type(k ^ ((k >> 31) & 0x7FFFFFFF), F32)


def _dsa_kernel(qi_ref, ki_ref, wi_ref, qc_ref, kc_ref, vt_ref, tri_ref, ztab_ref, wtab_ref, slope_ref, o_ref,
                sc_ref, sel_ref, ist_ref, fst_ref, m_ref, acc_ref):
    qb = pl.program_id(1)
    nq, ch = DSA_Q, DSA_CH
    g = ch // SUBLANES
    q0 = qb * nq
    nk = (q0 + nq + ch - 1) // ch
    key_i = lax.broadcasted_iota(I32, (ch, nq), 0)
    qry_i = lax.broadcasted_iota(I32, (ch, nq), 1)
    rel = key_i - qry_i

    def rows(a):
        return jnp.broadcast_to(a[:1], (SUBLANES, nq))

    def groups(a):
        return a.reshape(g, SUBLANES, nq)

    def gsum(a):
        return jnp.sum(jnp.sum(a.reshape(SUBLANES, g // SUBLANES, SUBLANES, nq), axis=1), axis=0)

    def gmax(a):
        return jnp.max(jnp.max(a.reshape(SUBLANES, g // SUBLANES, SUBLANES, nq), axis=1), axis=0)

    def colsum(a):
        return rows(jnp.sum(gsum(groups(a)), axis=0, keepdims=True))

    w_t = wi_ref[...].T
    w_rows = [w_t[h:h + 1, :] for h in range(IDX_HEADS)]

    def score_body(c, carry):
        s1, s2 = carry
        ks = pl.ds(pl.multiple_of(c * ch, ch), ch)
        k_lo = ki_ref[ks, :LANES]
        k_hi = ki_ref[ks, LANES:]
        sc = jnp.zeros((ch, nq), F32)
        for j in range(IDX_HEADS // 2):
            qj = qi_ref[:, j * LANES:(j + 1) * LANES]
            sc = sc + w_rows[2 * j] * jnp.maximum(_dot_nt(k_lo, qj), 0.0)
            sc = sc + w_rows[2 * j + 1] * jnp.maximum(_dot_nt(k_hi, qj), 0.0)
        causal = rel <= q0 - c * ch
        sc_ref[c] = groups(jnp.where(causal, sc, -jnp.inf))
        kept = jnp.where(causal, sc, 0.0)
        return s1 + colsum(kept), s2 + colsum(kept * kept)

    zeros = jnp.zeros((SUBLANES, nq), F32)
    s1, s2 = lax.fori_loop(0, nk, score_body, (zeros, zeros))

    def count(*preds):
        def body(c, accs):
            s = sc_ref[c]
            return tuple(a + gsum(jnp.where(p(s), 1.0, 0.0)) for a, p in zip(accs, preds))
        accs = lax.fori_loop(0, nk, body, (zeros,) * len(preds))
        return tuple(rows(jnp.sum(a, axis=0, keepdims=True)) for a in accs)

    topk = float(TOPK_MAX)
    target = topk + 0.5
    T_KEY, LO, HI = 0, 1, 2
    C_LO, C_HI, REAL_LO, REAL_HI, SIDE, THR, C_THR, DONE, STEP = range(9)
    fst_ref[THR] = jnp.full((SUBLANES, nq), -FLT_MAX, F32)
    fst_ref[C_THR] = zeros

    @pl.when(q0 + 1 > TOPK_MAX)
    def _():
        n = (q0 + 1 + lax.broadcasted_iota(I32, (SUBLANES, nq), 1)).astype(F32)
        mu = s1 / n
        sigma = jnp.sqrt(jnp.maximum(s2 / n - mu * mu, 0.0))
        fst_ref[STEP] = wtab_ref[...] * sigma * 1.5
        c_zero, c_pos = count(lambda s: s >= 0.0, lambda s: s > 0.0)
        zero_tie = (c_pos < topk) & (c_zero >= topk)
        above = c_pos >= topk
        below = c_zero < topk
        key0 = jnp.zeros((SUBLANES, nq), I32)
        lo0 = jnp.where(above, key0, jnp.full((SUBLANES, nq), _KEY_NEG_MAX, I32))
        hi0 = jnp.where(below, key0, jnp.full((SUBLANES, nq), _KEY_FIRST_NAN, I32))
        ist_ref[LO] = lo0
        ist_ref[HI] = hi0
        ist_ref[T_KEY] = jnp.clip(_sortable(mu + ztab_ref[...] * sigma), lo0 + 1, hi0 - 1)
        fst_ref[C_LO] = jnp.where(above, c_zero, n)
        fst_ref[C_HI] = jnp.where(below, c_zero, 0.0)
        fst_ref[REAL_LO] = jnp.where(above, 1.0, 0.0)
        fst_ref[REAL_HI] = jnp.where(below, 1.0, 0.0)
        fst_ref[SIDE] = zeros
        fst_ref[THR] = jnp.where(zero_tie, 0.0, -FLT_MAX)
        fst_ref[C_THR] = jnp.where(zero_tie, c_zero, 0.0)
        done0 = jnp.where(zero_tie, 1.0, 0.0)
        fst_ref[DONE] = done0

        def cond(st):
            return jnp.logical_and(st[0] < 60, st[1] > 0.0)

        def body(st):
            it = st[0]
            t_key, lo, hi = ist_ref[T_KEY], ist_ref[LO], ist_ref[HI]
            c_lo, c_hi, done, side = fst_ref[C_LO], fst_ref[C_HI], fst_ref[DONE], fst_ref[SIDE]
            t = _unsortable(t_key)
            (c,) = count(lambda s: s >= t[None])
            live = done == 0.0
            hit = live & (c == topk)
            up = live & (c > topk)
            dn = live & (c < topk)
            moved = jnp.where(up, 1.0, jnp.where(dn, -1.0, 0.0))
            repeat = (moved == side) & (moved != 0.0)
            lo = jnp.where(up, t_key, lo)
            c_lo = jnp.where(up, c, c_lo)
            real_lo = jnp.where(up, 1.0, fst_ref[REAL_LO])
            hi = jnp.where(dn, t_key, hi)
            c_hi = jnp.where(dn, c, c_hi)
            real_hi = jnp.where(dn, 1.0, fst_ref[REAL_HI])
            gap = lax.shift_right_logical(hi - lo, 1)
            shut = (gap == 0) & live & jnp.logical_not(hit)
            fst_ref[THR] = jnp.where(hit, t, jnp.where(shut, _unsortable(lo), fst_ref[THR]))
            fst_ref[C_THR] = jnp.where(hit, c, jnp.where(shut, c_lo, fst_ref[C_THR]))
            done = jnp.where(hit | shut, 1.0, done)
            lo_f, hi_f = _unsortable(lo), _unsortable(hi)
            both = real_lo * real_hi
            frac = jnp.where(repeat, 0.5, (c_lo - target) / (c_lo - c_hi))
            inside = lo_f + (hi_f - lo_f) * frac
            guess = jnp.where(both > 0.0, inside, t + fst_ref[STEP] * (c - target))
            g_key = jnp.clip(_sortable(guess), lo + 1, hi - 1)
            late = jnp.where(it >= 14, 1.0, 0.0)
            ist_ref[T_KEY] = jnp.where(late > 0.0, lo + gap, g_key)
            ist_ref[LO] = lo
            ist_ref[HI] = hi
            fst_ref[C_LO] = c_lo
            fst_ref[C_HI] = c_hi
            fst_ref[REAL_LO] = real_lo
            fst_ref[REAL_HI] = real_hi
            fst_ref[SIDE] = jnp.where(moved != 0.0, moved, side)
            fst_ref[DONE] = done
            return it + 1, float(nq) - jnp.sum(done[:1])

        lax.while_loop(cond, body, (jnp.int32(0), float(nq) - jnp.sum(done0[:1])))

    thr = fst_ref[THR]
    has_tie = jnp.max(fst_ref[C_THR]) > topk

    @pl.when(jnp.logical_not(has_tie))
    def _():
        def body(c, carry):
            sel_ref[c] = jnp.where(sc_ref[c] >= thr[None], 0.0, MASKED)
            return carry
        lax.fori_loop(0, nk, body, 0)

    @pl.when(has_tie)
    def _():
        (n_gt,) = count(lambda s: s > thr[None])
        need = topk - n_gt

        def body(c, seen):
            s = sc_ref[c]
            eq = s == thr[None]
            ones = jnp.where(eq, 1.0, 0.0).reshape(ch, nq).astype(BF16)
            rank = groups(_dot(tri_ref[...], ones)) + seen[None]
            take = (s > thr[None]) | (eq & (rank <= need[None]))
            sel_ref[c] = jnp.where(take, 0.0, MASKED)
            return rows(rank[g - 1, SUBLANES - 1:SUBLANES, :])
        lax.fori_loop(0, nk, body, zeros)

    m_ref[...] = jnp.full(m_ref.shape, MASKED, F32)
    acc_ref[...] = jnp.zeros(acc_ref.shape, F32)
    lane4 = lax.broadcasted_iota(I32, (ch, C_WIDTH), 1)
    relf = rel.astype(F32)
    ones_rows = jnp.ones((DSA_ONES, ch), BF16)
    ag = acc_ref.shape[1] // SUBLANES

    def attn_body(c, carry):
        ks = pl.ds(pl.multiple_of(c * ch, ch), ch)
        dist = relf + (c * ch - q0).astype(F32)
        sel = sel_ref[c].reshape(ch, nq)
        k = kc_ref[ks, :]
        zero = jnp.zeros_like(k)
        k4 = jnp.concatenate([jnp.where((lane4 >= h * HEAD_DIM) & (lane4 < (h + 1) * HEAD_DIM), k, zero)
                              for h in range(C_HEADS)], axis=0)
        s4 = _dot_nt(k4, qc_ref[...])
        prs, alphas = [], []
        for h in range(C_HEADS):
            s = groups(s4[h * ch:(h + 1) * ch] + (slope_ref[h] * dist + sel))
            m_prev = m_ref[h]
            m_new = jnp.maximum(m_prev, rows(jnp.max(gmax(s), axis=0, keepdims=True)))
            alphas.append(jnp.exp2(m_prev - m_new))
            prs.append(jnp.exp2(s - m_new[None]).reshape(ch, nq).astype(BF16))
            m_ref[h] = m_new
        for p in range(C_HEADS // 2):
            v_t = jnp.concatenate([vt_ref[p * LANES:(p + 1) * LANES, ks], ones_rows], axis=0)
            pv = _dot(v_t, jnp.concatenate(prs[2 * p:2 * p + 2], axis=1))
            for hh in range(2):
                h = 2 * p + hh
                acc = acc_ref[h].reshape(ag, SUBLANES, nq) * alphas[h][None]
                acc_ref[h] = acc.reshape(ag * SUBLANES, nq) + pv[:, hh * nq:(hh + 1) * nq]
        return carry

    lax.fori_loop(0, nk, attn_body, 0)
    dim = lax.broadcasted_iota(I32, (LANES, nq), 0)
    for p in range(C_HEADS // 2):
        outs = []
        for hh in range(2):
            acc = acc_ref[2 * p + hh]
            den = acc[LANES:LANES + SUBLANES]
            outs.append((acc[:LANES].reshape(LANES // SUBLANES, SUBLANES, nq) / den[None]).reshape(LANES, nq))
        o_ref[:, p * LANES:(p + 1) * LANES] = jnp.where(dim < HEAD_DIM, outs[0], outs[1]).T


def _quantile_table(s_len):
    n = jnp.arange(1, s_len + 1, dtype=F32)
    tail = jnp.clip((TOPK_MAX + 0.5) / n, 1e-6, 1.0 - 1e-6)
    z = jax.scipy.special.ndtri(1.0 - tail)
    pdf = jnp.exp(-0.5 * z * z) * (2.0 * np.pi) ** -0.5
    rep = lambda a: jnp.broadcast_to(a[None, :], (SUBLANES, s_len))
    return rep(z), rep(1.0 / (n * pdf))


def _dsa_call(qi, ki, wi, qc, kc, vc_t, slopes_c):
    bsz, s_len, _ = qc.shape
    nq, ch = DSA_Q, DSA_CH
    tri = (jnp.arange(ch)[:, None] >= jnp.arange(ch)[None, :]).astype(BF16)
    ztab, wtab = _quantile_table(s_len)
    blk = lambda w: pl.BlockSpec((None, nq, w), lambda b, i: (b, i, 0))
    seq = lambda w: pl.BlockSpec((None, s_len, w), lambda b, i: (b, 0, 0))
    tab = pl.BlockSpec((SUBLANES, nq), lambda b, i: (0, i))
    return pl.pallas_call(
        _dsa_kernel,
        grid=(bsz, s_len // nq),
        in_specs=[blk(IDX_HEADS * IDX_DIM), seq(2 * LANES), blk(LANES),
                  blk(C_WIDTH), seq(C_WIDTH),
                  pl.BlockSpec((None, C_WIDTH, s_len), lambda b, i: (b, 0, 0)),
                  pl.BlockSpec((ch, ch), lambda b, i: (0, 0)),
                  tab, tab,
                  pl.BlockSpec(memory_space=pltpu.SMEM)],
        out_specs=blk(C_WIDTH),
        out_shape=jax.ShapeDtypeStruct((bsz, s_len, C_WIDTH), F32),
        scratch_shapes=[pltpu.VMEM((s_len // ch, ch // SUBLANES, SUBLANES, nq), F32),
                        pltpu.VMEM((s_len // ch, ch // SUBLANES, SUBLANES, nq), F32),
                        pltpu.VMEM((3, SUBLANES, nq), I32),
                        pltpu.VMEM((9, SUBLANES, nq), F32),
                        pltpu.VMEM((C_HEADS, SUBLANES, nq), F32),
                        pltpu.VMEM((C_HEADS, LANES + DSA_ONES, nq), F32)],
        compiler_params=_cparams(("parallel", "arbitrary")),
    )(qi, ki, wi, qc, kc, vc_t, tri, ztab, wtab, slopes_c * LOG2E)


def _out_kernel(alpha, x_ref, mod_ref, oa_ref, ob_ref, oc_ref, gate_ref, wo_ref, g_ref, b_ref, y_ref):
    pg = gate_ref[...]
    sg = pg * jax.nn.sigmoid(pg)
    ya = (oa_ref[...] * sg[:, :A_WIDTH]).astype(BF16)
    yb = (ob_ref[...] * sg[:, A_WIDTH:A_WIDTH + B_WIDTH]).astype(BF16)
    yc = (oc_ref[...] * sg[:, A_WIDTH + B_WIDTH:]).astype(BF16)
    sub = (_dot(ya, wo_ref[:A_WIDTH, :]) + _dot(yb, wo_ref[A_WIDTH:A_WIDTH + B_WIDTH, :])
           + _dot(yc, wo_ref[A_WIDTH + B_WIDTH:, :]))
    gate = mod_ref[...][:, 2 * D_MODEL:]
    z = alpha * x_ref[...] + (1.0 + gate) * sub
    mu = jnp.mean(z, axis=-1, keepdims=True)
    zc = z - mu
    var = jnp.mean(zc * zc, axis=-1, keepdims=True)
    y_ref[...] = zc * lax.rsqrt(var + 1e-5) * g_ref[...] + b_ref[...]


def _out_call(alpha, x, mod, oa, ob, oc, gate, wo, g, b):
    bsz, s_len, d = x.shape
    tm = PROJ_ROWS
    rows = lambda w: pl.BlockSpec((None, tm, w), lambda bb, i: (bb, i, 0))
    full = lambda a: pl.BlockSpec(a.shape, lambda bb, i: (0,) * a.ndim)
    return pl.pallas_call(
        functools.partial(_out_kernel, alpha),
        grid=(bsz, s_len // tm),
        in_specs=[rows(d), pl.BlockSpec((None, 1, 3 * d), lambda bb, i: (bb, 0, 0)),
                  rows(A_WIDTH), rows(B_WIDTH), rows(C_WIDTH), rows(MIX_WIDTH),
                  full(wo), full(g), full(b)],
        out_specs=rows(d),
        out_shape=jax.ShapeDtypeStruct((bsz, s_len, d), F32),
        compiler_params=_cparams(("parallel", "parallel")),
    )(x, mod, oa, ob, oc, gate, wo, g, b)


def _pad_cols(a, width):
    return jnp.pad(a, ((0, 0), (0, width - a.shape[1])))


def _swap_halves(a):
    half = a.shape[1] // 2
    return jnp.concatenate([a[:, half:], a[:, :half]], axis=1)


def _layer_weights(w_in, w_uq, w_uk, w_uv, w_out):
    d = w_in.shape[0]
    z = lambda n: jnp.zeros((d, n), w_in.dtype)
    w_kr = w_in[:, O_KR:O_QB]
    w1 = jnp.concatenate([w_in[:, O_CQ:O_KR],
                          z(A_NOPE), w_kr, z(LANES - A_NOPE - A_ROPE),
                          z(A_NOPE), _swap_halves(w_kr), z(LANES - A_NOPE - A_ROPE)], axis=1)
    hq = A_NOPE + A_ROPE
    q1, q2 = [], []
    for h in range(A_HEADS):
        blk = w_uq[:, h * hq:(h + 1) * hq]
        q1.append(_pad_cols(blk, LANES))
        zq = jnp.zeros((w_uq.shape[0], A_NOPE), w_uq.dtype)
        q2.append(_pad_cols(jnp.concatenate([zq, _swap_halves(blk[:, A_NOPE:])], axis=1), LANES))
    wq1 = jnp.concatenate(q1, axis=1)
    wq2 = jnp.concatenate(q2, axis=1)
    wk = jnp.concatenate([_pad_cols(w_uk[:, h * A_NOPE:(h + 1) * A_NOPE], LANES)
                          for h in range(A_HEADS)], axis=1)
    w_ki = w_in[:, O_KI:O_WI]
    w2 = jnp.concatenate([w_in[:, O_QB:O_KI],
                          w_ki, z(LANES - IDX_DIM), z(LANES - IDX_DIM), w_ki,
                          w_in[:, O_WI:O_GATE], z(LANES - IDX_HEADS),
                          w_in[:, O_GATE:]], axis=1)
    bf = lambda a: a.astype(BF16)
    wv_t = w_in[:, O_VC:O_QI].T
    return bf(w1), bf(wq1), bf(wq2), bf(wk), bf(w_uv), bf(w2), bf(wv_t), bf(w_out)


def _rope_tables(s_len):
    pos = jnp.arange(s_len, dtype=F32)
    freqs = ROPE_THETA ** (-jnp.arange(0, A_ROPE, 2, dtype=F32) / A_ROPE)
    ang = pos[:, None] * freqs[None, :]
    cos, sin = jnp.cos(ang), jnp.sin(ang)
    ones = jnp.ones((s_len, A_NOPE), F32)
    zeros = jnp.zeros((s_len, LANES - A_NOPE - A_ROPE), F32)
    ct = jnp.concatenate([ones, cos, cos, zeros], axis=1)
    st = jnp.concatenate([jnp.zeros((s_len, A_NOPE), F32), -sin, sin, zeros], axis=1)
    return ct, st


def kernel(x, c, w_ada, b_ada, w_in, q_norm_g, kv_norm_g, w_uq, w_uk, w_uv, w_out, ln_g, ln_b):
    bsz, s_len, d = x.shape
    depth = w_ada.shape[0]
    alpha = (2 * depth) ** 0.25
    slopes = 2.0 ** (-8.0 * jnp.arange(1, N_ALIBI + 1, dtype=F32) / N_ALIBI)
    slopes_b, slopes_c = slopes[:B_HEADS], slopes[B_HEADS:]
    ct, st = _rope_tables(s_len)
    mod_all = _mod_call(c, w_ada, b_ada)

    for l in range(depth):
        w1, wq1, wq2, wk, wv, w2, wv_t, wo = _layer_weights(w_in[l], w_uq[l], w_uk[l], w_uv[l], w_out[l])
        mod = mod_all[l].reshape(bsz, 1, 3 * d)
        qa, ka, va = _proj_mla_call(x, mod, w1, q_norm_g[l].reshape(1, -1), kv_norm_g[l].reshape(1, -1),
                                    wq1, wq2, wk, wv, ct, st)
        qb, kb, vb, qc, kc, vc_t, qi, ki, wi, gate = _proj_rest_call(x, mod, w2, wv_t)
        o_a = _mla_call(qa, ka, va)
        o_b = _dil_call(qb, kb, vb, slopes_b)
        o_c = _dsa_call(qi, ki, wi, qc, kc, vc_t, slopes_c)
        x = _out_call(alpha, x, mod, o_a, o_b, o_c, gate, wo,
                      ln_g[l].reshape(1, -1), ln_b[l].reshape(1, -1))
    return x
```

```python
import functools

import numpy as np
import jax
import jax.numpy as jnp
from jax import lax
from jax.experimental import pallas as pl
from jax.experimental.pallas import tpu as pltpu

F32 = jnp.float32
BF16 = jnp.bfloat16
I32 = jnp.int32

D_MODEL = 1024
HEAD_DIM = 64
A_HEADS = 6
A_NOPE = 64
A_ROPE = 32
A_V = 64
Q_LORA = 768
KV_LORA = 256
ROPE_THETA = 10000.0
B_HEADS = 6
DILATED_PATTERNS = ((128, 1), (512, 4), (2048, 16))
C_HEADS = 4
IDX_HEADS = 8
IDX_DIM = 64
TOPK_MAX = 256
A_WIDTH = A_HEADS * A_V
B_WIDTH = B_HEADS * HEAD_DIM
C_WIDTH = C_HEADS * HEAD_DIM
MIX_WIDTH = A_WIDTH + B_WIDTH + C_WIDTH
N_ALIBI = B_HEADS + C_HEADS
SPLIT_SIZES = (Q_LORA, KV_LORA, A_ROPE, B_WIDTH, B_WIDTH, B_WIDTH, C_WIDTH, C_WIDTH, C_WIDTH,
               IDX_HEADS * IDX_DIM, IDX_DIM, IDX_HEADS, MIX_WIDTH)
_OFFS = [0] + [int(v) for v in np.cumsum(SPLIT_SIZES)]
(O_CQ, O_CKV, O_KR, O_QB, O_KB, O_VB, O_QC, O_KC, O_VC, O_QI, O_KI, O_WI, O_GATE, _O_END) = _OFFS

LANES = 128
SUBLANES = 8
LOG2E = 1.4426950408889634
MASKED = -1e30
VMEM_LIMIT = 48 * 1024 * 1024

PROJ_ROWS = 512
MLA_TQ = 512
DIL_N = 128
DIL_UNROLL = 8
DIL_MIX_ROWS = 512
DSA_Q = 256
DSA_ONES = 16
DSA_CH = 512
FLT_MAX = float(np.finfo(np.float32).max)
_KEY_NEG_MAX = int(np.int32(np.uint32(0xFF7FFFFF) ^ np.uint32(0x7FFFFFFF)))
_KEY_FIRST_NAN = 0x7F800001


def _dot(a, b):
    return jnp.dot(a, b, preferred_element_type=F32)


def _dot_nt(a, b):
    return lax.dot_general(a, b, (((1,), (1,)), ((), ())), preferred_element_type=F32)


def _cparams(sem):
    return pltpu.CompilerParams(dimension_semantics=sem, vmem_limit_bytes=VMEM_LIMIT)


def _mod_kernel(c_ref, w_ref, b_ref, o_ref):
    c = c_ref[...]
    sc = (c * jax.nn.sigmoid(c)).astype(BF16)
    o_ref[0] = _dot(sc, w_ref[0].astype(BF16)) + b_ref[0]


def _mod_call(c, w_ada, b_ada):
    depth, d, n3 = w_ada.shape
    bsz = c.shape[0]
    tn = 1024
    return pl.pallas_call(
        _mod_kernel,
        grid=(depth, n3 // tn),
        in_specs=[pl.BlockSpec((bsz, d), lambda l, j: (0, 0)),
                  pl.BlockSpec((1, d, tn), lambda l, j: (l, 0, j)),
                  pl.BlockSpec((1, 1, tn), lambda l, j: (l, 0, j))],
        out_specs=pl.BlockSpec((1, bsz, tn), lambda l, j: (l, 0, j)),
        out_shape=jax.ShapeDtypeStruct((depth, bsz, n3), F32),
        compiler_params=_cparams(("arbitrary", "arbitrary")),
    )(c, w_ada, b_ada.reshape(depth, 1, n3))


def _modulate(x_ref, mod_ref):
    mod = mod_ref[...]
    shift = mod[:, :D_MODEL]
    scale = mod[:, D_MODEL:2 * D_MODEL]
    return (x_ref[...] * (1.0 + scale) + shift).astype(BF16)


def _rms(x, g, eps=1e-6):
    ms = jnp.mean(x * x, axis=-1, keepdims=True)
    return (x * lax.rsqrt(ms + eps) * g).astype(BF16)


def _proj_mla_kernel(x_ref, mod_ref, w1_ref, gq_ref, gkv_ref, wq1_ref, wq2_ref, wk_ref, wv_ref,
                     ct_ref, st_ref, q_ref, k_ref, v_ref):
    h = _modulate(x_ref, mod_ref)
    pa = _dot(h, w1_ref[...])
    cq = _rms(pa[:, :Q_LORA], gq_ref[...])
    ckv = _rms(pa[:, Q_LORA:Q_LORA + KV_LORA], gkv_ref[...])
    ct = ct_ref[...]
    st = st_ref[...]
    ct6 = jnp.concatenate([ct] * A_HEADS, axis=1)
    st6 = jnp.concatenate([st] * A_HEADS, axis=1)
    qscale = (A_NOPE + A_ROPE) ** -0.5 * LOG2E
    q = (_dot(cq, wq1_ref[...]) * ct6 + _dot(cq, wq2_ref[...]) * st6) * qscale
    q_ref[...] = q.astype(BF16)
    o = Q_LORA + KV_LORA
    kr = pa[:, o:o + LANES] * ct + pa[:, o + LANES:o + 2 * LANES] * st
    k = _dot(ckv, wk_ref[...]) + jnp.concatenate([kr] * A_HEADS, axis=1)
    k_ref[...] = k.astype(BF16)
    v_ref[...] = _dot(ckv, wv_ref[...]).astype(BF16)


def _proj_mla_call(x, mod, w1, gq, gkv, wq1, wq2, wk, wv, ct, st):
    bsz, s_len, d = x.shape
    tm = PROJ_ROWS
    hw = A_HEADS * LANES
    full = lambda a: pl.BlockSpec(a.shape, lambda b, i: (0,) * a.ndim)
    return pl.pallas_call(
        _proj_mla_kernel,
        grid=(bsz, s_len // tm),
        in_specs=[pl.BlockSpec((None, tm, d), lambda b, i: (b, i, 0)),
                  pl.BlockSpec((None, 1, 3 * d), lambda b, i: (b, 0, 0)),
                  full(w1), full(gq), full(gkv), full(wq1), full(wq2), full(wk), full(wv),
                  pl.BlockSpec((tm, LANES), lambda b, i: (i, 0)),
                  pl.BlockSpec((tm, LANES), lambda b, i: (i, 0))],
        out_specs=[pl.BlockSpec((None, tm, hw), lambda b, i: (b, i, 0)),
                   pl.BlockSpec((None, tm, hw), lambda b, i: (b, i, 0)),
                   pl.BlockSpec((None, tm, A_WIDTH), lambda b, i: (b, i, 0))],
        out_shape=[jax.ShapeDtypeStruct((bsz, s_len, hw), BF16),
                   jax.ShapeDtypeStruct((bsz, s_len, hw), BF16),
                   jax.ShapeDtypeStruct((bsz, s_len, A_WIDTH), BF16)],
        compiler_params=_cparams(("parallel", "parallel")),
    )(x, mod, w1, gq, gkv, wq1, wq2, wk, wv, ct, st)


_R_QB, _R_KB, _R_VB = 0, B_WIDTH, 2 * B_WIDTH
_R_QC = 3 * B_WIDTH
_R_KC, _R_VC = _R_QC + C_WIDTH, _R_QC + 2 * C_WIDTH
_R_QI = _R_QC + 3 * C_WIDTH
_R_KI = _R_QI + IDX_HEADS * IDX_DIM
_R_WI = _R_KI + 2 * LANES
_R_GATE = _R_WI + LANES
_R_END = _R_GATE + MIX_WIDTH


def _proj_rest_kernel(x_ref, mod_ref, w2_ref, wvt_ref, qb_ref, kb_ref, vb_ref, qc_ref, kc_ref, vct_ref,
                      qi_ref, ki_ref, wi_ref, gate_ref):
    h = _modulate(x_ref, mod_ref)

    def proj(start, width):
        return _dot(h, w2_ref[:, start:start + width])

    inv_sqrt_dh = HEAD_DIM ** -0.5 * LOG2E
    for ref, val in ((qb_ref, proj(_R_QB, B_WIDTH) * inv_sqrt_dh), (kb_ref, proj(_R_KB, B_WIDTH)),
                     (vb_ref, proj(_R_VB, B_WIDTH))):
        for p in range(B_HEADS // 2):
            ref[p] = val[:, p * LANES:(p + 1) * LANES]
    qc_ref[...] = (proj(_R_QC, C_WIDTH) * inv_sqrt_dh).astype(BF16)
    kc_ref[...] = proj(_R_KC, C_WIDTH).astype(BF16)
    vct_ref[...] = _dot_nt(wvt_ref[...], h).astype(BF16)
    qi_ref[...] = (proj(_R_QI, IDX_HEADS * IDX_DIM) * IDX_DIM ** -0.5).astype(BF16)
    ki_ref[...] = proj(_R_KI, 2 * LANES).astype(BF16)
    wi_ref[...] = proj(_R_WI, LANES) * IDX_HEADS ** -0.5
    gate_ref[...] = proj(_R_GATE, MIX_WIDTH)


def _proj_rest_call(x, mod, w2, wv_t):
    bsz, s_len, d = x.shape
    tm = PROJ_ROWS
    widths = (B_WIDTH, B_WIDTH, B_WIDTH, C_WIDTH, C_WIDTH, C_WIDTH, IDX_HEADS * IDX_DIM,
              2 * LANES, LANES, MIX_WIDTH)
    dtypes = (F32,) * 3 + (BF16,) * 5 + (F32, F32)
    slab = pl.BlockSpec((None, B_HEADS // 2, tm, LANES), lambda b, i: (b, 0, i, 0))
    slab_shape = (bsz, B_HEADS // 2, s_len, LANES)
    return pl.pallas_call(
        _proj_rest_kernel,
        grid=(bsz, s_len // tm),
        in_specs=[pl.BlockSpec((None, tm, d), lambda b, i: (b, i, 0)),
                  pl.BlockSpec((None, 1, 3 * d), lambda b, i: (b, 0, 0)),
                  pl.BlockSpec(w2.shape, lambda b, i: (0, 0)),
                  pl.BlockSpec(wv_t.shape, lambda b, i: (0, 0))],
        out_specs=[slab if n < 3 else pl.BlockSpec((None, C_WIDTH, tm), lambda b, i: (b, 0, i)) if n == 5 else
                   pl.BlockSpec((None, tm, w), lambda b, i: (b, i, 0)) for n, w in enumerate(widths)],
        out_shape=[jax.ShapeDtypeStruct(slab_shape if n < 3 else (bsz, C_WIDTH, s_len) if n == 5 else
                                        (bsz, s_len, w), dt) for n, (w, dt) in enumerate(zip(widths, dtypes))],
        compiler_params=_cparams(("parallel", "parallel")),
    )(x, mod, w2, wv_t)


def _with_ones(v):
    return jnp.concatenate([v, jnp.ones_like(v)], axis=1)


def _init_state(m_ref, acc_ref):
    m_ref[...] = jnp.full(m_ref.shape, MASKED, F32)
    acc_ref[...] = jnp.zeros(acc_ref.shape, F32)


def _pair_output(acc_ref, i0, i1):
    lane = lax.broadcasted_iota(I32, (acc_ref.shape[1], LANES), 1)
    a0, a1 = acc_ref[i0], acc_ref[i1]
    o0 = a0[:, :LANES] / a0[:, LANES:]
    o1 = a1[:, :LANES] / a1[:, LANES:]
    return jnp.where(lane < HEAD_DIM, o0, o1)


def _mla_kernel(q_ref, k_ref, v_ref, o_ref, m_ref, acc_ref):
    qi = pl.program_id(2)
    tq = q_ref.shape[0]
    _init_state(m_ref, acc_ref)
    lane = lax.broadcasted_iota(I32, (tq, 2 * LANES), 1)
    row = lax.broadcasted_iota(I32, (tq, tq), 0)
    col = lax.broadcasted_iota(I32, (tq, tq), 1)

    def step(c, diagonal):
        ks = pl.ds(pl.multiple_of(c * tq, tq), tq)
        k = k_ref[ks, :]
        zero = jnp.zeros_like(k)
        k2 = jnp.concatenate([jnp.where(lane < LANES, k, zero), jnp.where(lane >= LANES, k, zero)], axis=0)
        s2 = _dot_nt(q_ref[...], k2)
        ps, alphas = [], []
        for hh in range(2):
            s = s2[:, hh * tq:(hh + 1) * tq]
            if diagonal:
                s = jnp.where(col <= row, s, MASKED)
            m_prev = m_ref[hh]
            m_new = jnp.maximum(m_prev, jnp.max(s, axis=1, keepdims=True))
            alphas.append(jnp.exp2(m_prev - m_new))
            ps.append(jnp.exp2(s - jnp.tile(m_new, (1, tq // LANES))).astype(BF16))
            m_ref[hh] = m_new
        pv = _dot(jnp.concatenate(ps, axis=0), _with_ones(v_ref[ks, :]))
        for hh in range(2):
            acc_ref[hh] = jnp.tile(alphas[hh], (1, 2)) * acc_ref[hh] + pv[hh * tq:(hh + 1) * tq]

    def body(c, carry):
        step(c, False)
        return carry

    lax.fori_loop(0, qi, body, 0)
    step(qi, True)
    o_ref[...] = _pair_output(acc_ref, 0, 1)


def _mla_call(q, k, v):
    bsz, s_len, _ = q.shape
    tq = MLA_TQ
    npair = A_HEADS // 2
    return pl.pallas_call(
        _mla_kernel,
        grid=(bsz, npair, s_len // tq),
        in_specs=[pl.BlockSpec((None, tq, 2 * LANES), lambda b, p, i: (b, i, p)),
                  pl.BlockSpec((None, s_len, 2 * LANES), lambda b, p, i: (b, 0, p)),
                  pl.BlockSpec((None, s_len, LANES), lambda b, p, i: (b, 0, p))],
        out_specs=pl.BlockSpec((None, tq, LANES), lambda b, p, i: (b, i, p)),
        out_shape=jax.ShapeDtypeStruct((bsz, s_len, A_WIDTH), F32),
        scratch_shapes=[pltpu.VMEM((2, tq, LANES), F32), pltpu.VMEM((2, tq, 2 * LANES), F32)],
        compiler_params=_cparams(("parallel", "parallel", "arbitrary")),
    )(q, k, v)


def _dil_kernel(q_ref, k_ref, v_ref, bias_ref, o_ref, oacc_ref, lse_ref):
    n = DIL_N
    s_len = q_ref.shape[0]
    lane = lax.broadcasted_iota(I32, (2 * n, LANES), 1)
    first = lax.broadcasted_iota(I32, (n, LANES), 1) < HEAD_DIM

    for pat, (_, dil) in enumerate(DILATED_PATTERNS):
        nb = s_len // dil // n
        stride = dil if dil > 1 else None

        def unit(t, pat=pat, dil=dil, nb=nb, stride=stride):
            r, j = t // nb, t % nb
            q_start = r + j * (n * dil)
            b_start = r + jnp.maximum(j - 1, 0) * (n * dil)
            table = jnp.minimum(j, 1)
            q = q_ref[pl.ds(q_start, n, stride=stride), :].astype(BF16)
            k = k_ref[pl.ds(b_start, 2 * n, stride=stride), :].astype(BF16)
            v = _with_ones(v_ref[pl.ds(b_start, 2 * n, stride=stride), :].astype(BF16))
            zero = jnp.zeros_like(k)
            k2 = jnp.concatenate([jnp.where(lane < HEAD_DIM, k, zero), jnp.where(lane >= HEAD_DIM, k, zero)],
                                 axis=0)
            s2 = _dot_nt(q, k2)
            ps, ms = [], []
            for hh in range(2):
                s = s2[:, hh * 2 * n:(hh + 1) * 2 * n] + bias_ref[pat, table, hh]
                m = jnp.max(s, axis=1, keepdims=True)
                ps.append(jnp.exp2(s - m).astype(BF16))
                ms.append(m)
            pv = _dot(jnp.concatenate(ps, axis=0), v)
            outs = [pv[hh * n:(hh + 1) * n, :LANES] / pv[hh * n:(hh + 1) * n, LANES:] for hh in range(2)]
            lses = [ms[hh] + jnp.log2(pv[hh * n:(hh + 1) * n, LANES:]) for hh in range(2)]
            oacc_ref[pat, pl.ds(q_start, n, stride=stride), :] = jnp.where(first, outs[0], outs[1])
            lse_ref[pat, pl.ds(q_start, n, stride=stride), :] = jnp.where(first, lses[0], lses[1])

        def body(i, carry, unit=unit):
            for u in range(DIL_UNROLL):
                unit(i * DIL_UNROLL + u)
            return carry

        lax.fori_loop(0, s_len // n // DIL_UNROLL, body, 0)

    def mix(i, carry):
        rs = pl.ds(pl.multiple_of(i * DIL_MIX_ROWS, DIL_MIX_ROWS), DIL_MIX_ROWS)
        ls = [lse_ref[p, rs, :] for p in range(len(DILATED_PATTERNS))]
        mx = functools.reduce(jnp.maximum, ls)
        es = [jnp.exp2(l - mx) for l in ls]
        num = functools.reduce(lambda a, b: a + b, [e * oacc_ref[p, rs, :] for p, e in enumerate(es)])
        o_ref[rs, :] = num / functools.reduce(lambda a, b: a + b, es)
        return carry

    lax.fori_loop(0, s_len // DIL_MIX_ROWS, mix, 0)


def _dil_bias(slopes_b):
    n = DIL_N
    kj = jnp.arange(2 * n)
    step = jnp.arange(n)[:, None] + n - kj[None, :]
    valid = (step >= 0) & (step <= n)
    tables = []
    for _, dil in DILATED_PATTERNS:
        bias = -(slopes_b * LOG2E)[:, None, None] * (step * dil).astype(F32)[None]
        banded = jnp.where(valid[None], bias, MASKED)
        first = jnp.concatenate([banded[:, :, n:], jnp.full_like(banded[:, :, n:], MASKED)], axis=2)
        tables.append(jnp.stack([first, banded]))
    return jnp.stack(tables).reshape(len(DILATED_PATTERNS), 2, B_HEADS // 2, 2, n, 2 * n)


def _dil_call(q, k, v, slopes_b):
    bsz, npair, s_len, _ = q.shape
    npat = len(DILATED_PATTERNS)
    for window, dil in DILATED_PATTERNS:
        assert window // dil == DIL_N and s_len % (dil * DIL_N * 2) == 0
    bias = _dil_bias(slopes_b)
    slab = pl.BlockSpec((None, None, s_len, LANES), lambda b, p: (b, p, 0, 0))
    return pl.pallas_call(
        _dil_kernel,
        grid=(bsz, npair),
        in_specs=[slab, slab, slab,
                  pl.BlockSpec((npat, 2, None, 2, DIL_N, 2 * DIL_N), lambda b, p: (0, 0, p, 0, 0, 0))],
        out_specs=pl.BlockSpec((None, s_len, LANES), lambda b, p: (b, 0, p)),
        out_shape=jax.ShapeDtypeStruct((bsz, s_len, npair * LANES), F32),
        scratch_shapes=[pltpu.VMEM((npat, s_len, LANES), F32), pltpu.VMEM((npat, s_len, LANES), F32)],
        compiler_params=_cparams(("parallel", "parallel")),
    )(q, k, v, bias)


def _sortable(x):
    b = lax.bitcast_convert_type(x, I32)
    return b ^ ((b >> 31) & 0x7FFFFFFF)


def _unsortable(k):
    return lax.bitcast_convert_type(k ^ ((k >> 31) & 0x7FFFFFFF), F32)


def _dsa_kernel(qi_ref, ki_ref, wi_ref, qc_ref, kc_ref, vt_ref, tri_ref, ztab_ref, wtab_ref, slope_ref, o_ref,
                sc_ref, sel_ref, ist_ref, fst_ref, m_ref, acc_ref):
    qb = pl.program_id(1)
    nq, ch = DSA_Q, DSA_CH
    g = ch // SUBLANES
    q0 = qb * nq
    nk = (q0 + nq + ch - 1) // ch
    key_i = lax.broadcasted_iota(I32, (ch, nq), 0)
    qry_i = lax.broadcasted_iota(I32, (ch, nq), 1)
    rel = key_i - qry_i

    def rows(a):
        return jnp.broadcast_to(a[:1], (SUBLANES, nq))

    def groups(a):
        return a.reshape(g, SUBLANES, nq)

    def gsum(a):
        return jnp.sum(jnp.sum(a.reshape(SUBLANES, g // SUBLANES, SUBLANES, nq), axis=1), axis=0)

    def gmax(a):
        return jnp.max(jnp.max(a.reshape(SUBLANES, g // SUBLANES, SUBLANES, nq), axis=1), axis=0)

    def gmin(a):
        return jnp.min(jnp.min(a.reshape(SUBLANES, g // SUBLANES, SUBLANES, nq), axis=1), axis=0)

    def colsum(a):
        return rows(jnp.sum(gsum(groups(a)), axis=0, keepdims=True))

    w_t = wi_ref[...].T
    w_rows = [w_t[h:h + 1, :] for h in range(IDX_HEADS)]

    def score_body(c, carry):
        s1, s2 = carry
        ks = pl.ds(pl.multiple_of(c * ch, ch), ch)
        k_lo = ki_ref[ks, :LANES]
        k_hi = ki_ref[ks, LANES:]
        sc = jnp.zeros((ch, nq), F32)
        for j in range(IDX_HEADS // 2):
            qj = qi_ref[:, j * LANES:(j + 1) * LANES]
            sc = sc + w_rows[2 * j] * jnp.maximum(_dot_nt(k_lo, qj), 0.0)
            sc = sc + w_rows[2 * j + 1] * jnp.maximum(_dot_nt(k_hi, qj), 0.0)
        causal = rel <= q0 - c * ch
        sc_ref[c] = groups(jnp.where(causal, sc, -jnp.inf))
        kept = jnp.where(causal, sc, 0.0)
        return s1 + colsum(kept), s2 + colsum(kept * kept)

    zeros = jnp.zeros((SUBLANES, nq), F32)
    s1, s2 = lax.fori_loop(0, nk, score_body, (zeros, zeros))

    def count(*preds):
        def body(c, accs):
            s = sc_ref[c]
            return tuple(a + gsum(jnp.where(p(s), 1.0, 0.0)) for a, p in zip(accs, preds))
        accs = lax.fori_loop(0, nk, body, (zeros,) * len(preds))
        return tuple(rows(jnp.sum(a, axis=0, keepdims=True)) for a in accs)

    topk = float(TOPK_MAX)
    target = topk + 0.5
    T_KEY, LO, HI = 0, 1, 2
    C_LO, C_HI, REAL_LO, REAL_HI, SIDE, THR, C_THR, DONE, STEP, NEAR = range(10)
    fst_ref[THR] = jnp.full((SUBLANES, nq), -FLT_MAX, F32)
    fst_ref[C_THR] = zeros

    @pl.when(q0 + 1 > TOPK_MAX)
    def _():
        n = (q0 + 1 + lax.broadcasted_iota(I32, (SUBLANES, nq), 1)).astype(F32)
        mu = s1 / n
        sigma = jnp.sqrt(jnp.maximum(s2 / n - mu * mu, 0.0))
        fst_ref[STEP] = wtab_ref[...] * sigma * 1.5
        c_zero, c_pos = count(lambda s: s >= 0.0, lambda s: s > 0.0)
        zero_tie = (c_pos < topk) & (c_zero >= topk)
        above = c_pos >= topk
        below = c_zero < topk
        key0 = jnp.zeros((SUBLANES, nq), I32)
        lo0 = jnp.where(above, key0, jnp.full((SUBLANES, nq), _KEY_NEG_MAX, I32))
        hi0 = jnp.where(below, key0, jnp.full((SUBLANES, nq), _KEY_FIRST_NAN, I32))
        ist_ref[LO] = lo0
        ist_ref[HI] = hi0
        ist_ref[T_KEY] = jnp.clip(_sortable(mu + ztab_ref[...] * sigma), lo0 + 1, hi0 - 1)
        fst_ref[C_LO] = jnp.where(above, c_zero, n)
        fst_ref[C_HI] = jnp.where(below, c_zero, 0.0)
        fst_ref[REAL_LO] = jnp.where(above, 1.0, 0.0)
        fst_ref[REAL_HI] = jnp.where(below, 1.0, 0.0)
        fst_ref[SIDE] = zeros
        fst_ref[NEAR] = zeros
        fst_ref[THR] = jnp.where(zero_tie, 0.0, -FLT_MAX)
        fst_ref[C_THR] = jnp.where(zero_tie, c_zero, 0.0)
        done0 = jnp.where(zero_tie, 1.0, 0.0)
        fst_ref[DONE] = done0

        def cond(st):
            return jnp.logical_and(st[0] < 60, st[1] > 0.0)

        def body(st):
            it = st[0]
            t_key, lo, hi = ist_ref[T_KEY], ist_ref[LO], ist_ref[HI]
            c_lo, c_hi, done, side = fst_ref[C_LO], fst_ref[C_HI], fst_ref[DONE], fst_ref[SIDE]
            t = _unsortable(t_key)
            (c,) = count(lambda s: s >= t[None])
            live = done == 0.0
            hit = live & (jnp.abs(c - topk) <= 1.0)
            up = live & (c > topk + 1.0)
            dn = live & (c < topk - 1.0)
            moved = jnp.where(up, 1.0, jnp.where(dn, -1.0, 0.0))
            repeat = (moved == side) & (moved != 0.0)
            lo = jnp.where(up, t_key, lo)
            c_lo = jnp.where(up, c, c_lo)
            real_lo = jnp.where(up, 1.0, fst_ref[REAL_LO])
            hi = jnp.where(dn, t_key, hi)
            c_hi = jnp.where(dn, c, c_hi)
            real_hi = jnp.where(dn, 1.0, fst_ref[REAL_HI])
            gap = lax.shift_right_logical(hi - lo, 1)
            shut = (gap == 0) & live & jnp.logical_not(hit)
            fst_ref[THR] = jnp.where(hit, t, jnp.where(shut, _unsortable(lo), fst_ref[THR]))
            fst_ref[C_THR] = jnp.where(hit, c, jnp.where(shut, c_lo, fst_ref[C_THR]))
            fst_ref[NEAR] = jnp.where(hit, 1.0, fst_ref[NEAR])
            done = jnp.where(hit | shut, 1.0, done)
            lo_f, hi_f = _unsortable(lo), _unsortable(hi)
            both = real_lo * real_hi
            frac = jnp.where(repeat, 0.5, (c_lo - target) / (c_lo - c_hi))
            inside = lo_f + (hi_f - lo_f) * frac
            guess = jnp.where(both > 0.0, inside, t + fst_ref[STEP] * (c - target))
            g_key = jnp.clip(_sortable(guess), lo + 1, hi - 1)
            late = jnp.where(it >= 14, 1.0, 0.0)
            ist_ref[T_KEY] = jnp.where(late > 0.0, lo + gap, g_key)
            ist_ref[LO] = lo
            ist_ref[HI] = hi
            fst_ref[C_LO] = c_lo
            fst_ref[C_HI] = c_hi
            fst_ref[REAL_LO] = real_lo
            fst_ref[REAL_HI] = real_hi
            fst_ref[SIDE] = jnp.where(moved != 0.0, moved, side)
            fst_ref[DONE] = done
            return it + 1, float(nq) - jnp.sum(done[:1])

        lax.while_loop(cond, body, (jnp.int32(0), float(nq) - jnp.sum(done0[:1])))

        t, c_t, near = fst_ref[THR], fst_ref[C_THR], fst_ref[NEAR]
        short = (near > 0.0) & (c_t == topk - 1.0)
        extra = (near > 0.0) & (c_t == topk + 1.0)
        inf = jnp.full((SUBLANES, nq), jnp.inf, F32)

        def around(c, carry):
            lower, upper = carry
            s = sc_ref[c]
            below = s < t[None]
            return (jnp.maximum(lower, gmax(jnp.where(below, s, -jnp.inf))),
                    jnp.minimum(upper, gmin(jnp.where(below, jnp.inf, s))))

        lower, upper = lax.fori_loop(0, nk, around, (-inf, inf))
        lower = rows(jnp.max(lower, axis=0, keepdims=True))
        upper = rows(jnp.min(upper, axis=0, keepdims=True))
        x = jnp.where(short, lower, jnp.where(extra, upper, t))

        def recount(c, carry):
            n_ge, n_gt, nxt = carry
            s = sc_ref[c]
            gt = s > x[None]
            return (n_ge + gsum(jnp.where(s >= x[None], 1.0, 0.0)), n_gt + gsum(jnp.where(gt, 1.0, 0.0)),
                    jnp.minimum(nxt, gmin(jnp.where(gt, s, jnp.inf))))

        n_ge, n_gt, nxt = lax.fori_loop(0, nk, recount, (zeros, zeros, inf))
        n_ge = rows(jnp.sum(n_ge, axis=0, keepdims=True))
        n_gt = rows(jnp.sum(n_gt, axis=0, keepdims=True))
        nxt = rows(jnp.min(nxt, axis=0, keepdims=True))
        step_up = extra & (n_gt == topk)
        fst_ref[THR] = jnp.where(step_up, nxt, x)
        fst_ref[C_THR] = jnp.where(step_up, n_gt, n_ge)

    thr = fst_ref[THR]
    has_tie = jnp.max(fst_ref[C_THR]) > topk

    @pl.when(jnp.logical_not(has_tie))
    def _():
        def body(c, carry):
            sel_ref[c] = jnp.where(sc_ref[c] >= thr[None], 0.0, MASKED)
            return carry
        lax.fori_loop(0, nk, body, 0)

    @pl.when(has_tie)
    def _():
        (n_gt,) = count(lambda s: s > thr[None])
        need = topk - n_gt

        def body(c, seen):
            s = sc_ref[c]
            eq = s == thr[None]
            ones = jnp.where(eq, 1.0, 0.0).reshape(ch, nq).astype(BF16)
            rank = groups(_dot(tri_ref[...], ones)) + seen[None]
            take = (s > thr[None]) | (eq & (rank <= need[None]))
            sel_ref[c] = jnp.where(take, 0.0, MASKED)
            return rows(rank[g - 1, SUBLANES - 1:SUBLANES, :])
        lax.fori_loop(0, nk, body, zeros)

    m_ref[...] = jnp.full(m_ref.shape, MASKED, F32)
    acc_ref[...] = jnp.zeros(acc_ref.shape, F32)
    lane4 = lax.broadcasted_iota(I32, (ch, C_WIDTH), 1)
    relf = rel.astype(F32)
    ones_rows = jnp.ones((DSA_ONES, ch), BF16)
    ag = acc_ref.shape[1] // SUBLANES

    def attn_body(c, carry):
        ks = pl.ds(pl.multiple_of(c * ch, ch), ch)
        dist = relf + (c * ch - q0).astype(F32)
        sel = sel_ref[c].reshape(ch, nq)
        k = kc_ref[ks, :]
        zero = jnp.zeros_like(k)
        k4 = jnp.concatenate([jnp.where((lane4 >= h * HEAD_DIM) & (lane4 < (h + 1) * HEAD_DIM), k, zero)
                              for h in range(C_HEADS)], axis=0)
        s4 = _dot_nt(k4, qc_ref[...])
        prs, alphas = [], []
        for h in range(C_HEADS):
            s = groups(s4[h * ch:(h + 1) * ch] + (slope_ref[h] * dist + sel))
            m_prev = m_ref[h]
            m_new = jnp.maximum(m_prev, rows(jnp.max(gmax(s), axis=0, keepdims=True)))
            alphas.append(jnp.exp2(m_prev - m_new))
            prs.append(jnp.exp2(s - m_new[None]).reshape(ch, nq).astype(BF16))
            m_ref[h] = m_new
        for p in range(C_HEADS // 2):
            v_t = jnp.concatenate([vt_ref[p * LANES:(p + 1) * LANES, ks], ones_rows], axis=0)
            pv = _dot(v_t, jnp.concatenate(prs[2 * p:2 * p + 2], axis=1))
            for hh in range(2):
                h = 2 * p + hh
                acc = acc_ref[h].reshape(ag, SUBLANES, nq) * alphas[h][None]
                acc_ref[h] = acc.reshape(ag * SUBLANES, nq) + pv[:, hh * nq:(hh + 1) * nq]
        return carry

    lax.fori_loop(0, nk, attn_body, 0)
    dim = lax.broadcasted_iota(I32, (LANES, nq), 0)
    for p in range(C_HEADS // 2):
        outs = []
        for hh in range(2):
            acc = acc_ref[2 * p + hh]
            den = acc[LANES:LANES + SUBLANES]
            outs.append((acc[:LANES].reshape(LANES // SUBLANES, SUBLANES, nq) / den[None]).reshape(LANES, nq))
        o_ref[:, p * LANES:(p + 1) * LANES] = jnp.where(dim < HEAD_DIM, outs[0], outs[1]).T


def _quantile_table(s_len):
    n = jnp.arange(1, s_len + 1, dtype=F32)
    tail = jnp.clip((TOPK_MAX + 0.5) / n, 1e-6, 1.0 - 1e-6)
    z = jax.scipy.special.ndtri(1.0 - tail)
    pdf = jnp.exp(-0.5 * z * z) * (2.0 * np.pi) ** -0.5
    rep = lambda a: jnp.broadcast_to(a[None, :], (SUBLANES, s_len))
    return rep(z), rep(1.0 / (n * pdf))


def _dsa_call(qi, ki, wi, qc, kc, vc_t, slopes_c):
    bsz, s_len, _ = qc.shape
    nq, ch = DSA_Q, DSA_CH
    tri = (jnp.arange(ch)[:, None] >= jnp.arange(ch)[None, :]).astype(BF16)
    ztab, wtab = _quantile_table(s_len)
    blk = lambda w: pl.BlockSpec((None, nq, w), lambda b, i: (b, i, 0))
    seq = lambda w: pl.BlockSpec((None, s_len, w), lambda b, i: (b, 0, 0))
    tab = pl.BlockSpec((SUBLANES, nq), lambda b, i: (0, i))
    return pl.pallas_call(
        _dsa_kernel,
        grid=(bsz, s_len // nq),
        in_specs=[blk(IDX_HEADS * IDX_DIM), seq(2 * LANES), blk(LANES),
                  blk(C_WIDTH), seq(C_WIDTH),
                  pl.BlockSpec((None, C_WIDTH, s_len), lambda b, i: (b, 0, 0)),
                  pl.BlockSpec((ch, ch), lambda b, i: (0, 0)),
                  tab, tab,
                  pl.BlockSpec(memory_space=pltpu.SMEM)],
        out_specs=blk(C_WIDTH),
        out_shape=jax.ShapeDtypeStruct((bsz, s_len, C_WIDTH), F32),
        scratch_shapes=[pltpu.VMEM((s_len // ch, ch // SUBLANES, SUBLANES, nq), F32),
                        pltpu.VMEM((s_len // ch, ch // SUBLANES, SUBLANES, nq), F32),
                        pltpu.VMEM((3, SUBLANES, nq), I32),
                        pltpu.VMEM((10, SUBLANES, nq), F32),
                        pltpu.VMEM((C_HEADS, SUBLANES, nq), F32),
                        pltpu.VMEM((C_HEADS, LANES + DSA_ONES, nq), F32)],
        compiler_params=_cparams(("parallel", "arbitrary")),
    )(qi, ki, wi, qc, kc, vc_t, tri, ztab, wtab, slopes_c * LOG2E)


def _out_kernel(alpha, x_ref, mod_ref, oa_ref, ob_ref, oc_ref, gate_ref, wo_ref, g_ref, b_ref, y_ref):
    pg = gate_ref[...]
    sg = pg * jax.nn.sigmoid(pg)
    ya = (oa_ref[...] * sg[:, :A_WIDTH]).astype(BF16)
    yb = (ob_ref[...] * sg[:, A_WIDTH:A_WIDTH + B_WIDTH]).astype(BF16)
    yc = (oc_ref[...] * sg[:, A_WIDTH + B_WIDTH:]).astype(BF16)
    sub = (_dot(ya, wo_ref[:A_WIDTH, :]) + _dot(yb, wo_ref[A_WIDTH:A_WIDTH + B_WIDTH, :])
           + _dot(yc, wo_ref[A_WIDTH + B_WIDTH:, :]))
    gate = mod_ref[...][:, 2 * D_MODEL:]
    z = alpha * x_ref[...] + (1.0 + gate) * sub
    mu = jnp.mean(z, axis=-1, keepdims=True)
    zc = z - mu
    var = jnp.mean(zc * zc, axis=-1, keepdims=True)
    y_ref[...] = zc * lax.rsqrt(var + 1e-5) * g_ref[...] + b_ref[...]


def _out_call(alpha, x, mod, oa, ob, oc, gate, wo, g, b):
    bsz, s_len, d = x.shape
    tm = PROJ_ROWS
    rows = lambda w: pl.BlockSpec((None, tm, w), lambda bb, i: (bb, i, 0))
    full = lambda a: pl.BlockSpec(a.shape, lambda bb, i: (0,) * a.ndim)
    return pl.pallas_call(
        functools.partial(_out_kernel, alpha),
        grid=(bsz, s_len // tm),
        in_specs=[rows(d), pl.BlockSpec((None, 1, 3 * d), lambda bb, i: (bb, 0, 0)),
                  rows(A_WIDTH), rows(B_WIDTH), rows(C_WIDTH), rows(MIX_WIDTH),
                  full(wo), full(g), full(b)],
        out_specs=rows(d),
        out_shape=jax.ShapeDtypeStruct((bsz, s_len, d), F32),
        compiler_params=_cparams(("parallel", "parallel")),
    )(x, mod, oa, ob, oc, gate, wo, g, b)


def _pad_cols(a, width):
    return jnp.pad(a, ((0, 0), (0, width - a.shape[1])))


def _swap_halves(a):
    half = a.shape[1] // 2
    return jnp.concatenate([a[:, half:], a[:, :half]], axis=1)


def _layer_weights(w_in, w_uq, w_uk, w_uv, w_out):
    d = w_in.shape[0]
    z = lambda n: jnp.zeros((d, n), w_in.dtype)
    w_kr = w_in[:, O_KR:O_QB]
    w1 = jnp.concatenate([w_in[:, O_CQ:O_KR],
                          z(A_NOPE), w_kr, z(LANES - A_NOPE - A_ROPE),
                          z(A_NOPE), _swap_halves(w_kr), z(LANES - A_NOPE - A_ROPE)], axis=1)
    hq = A_NOPE + A_ROPE
    q1, q2 = [], []
    for h in range(A_HEADS):
        blk = w_uq[:, h * hq:(h + 1) * hq]
        q1.append(_pad_cols(blk, LANES))
        zq = jnp.zeros((w_uq.shape[0], A_NOPE), w_uq.dtype)
        q2.append(_pad_cols(jnp.concatenate([zq, _swap_halves(blk[:, A_NOPE:])], axis=1), LANES))
    wq1 = jnp.concatenate(q1, axis=1)
    wq2 = jnp.concatenate(q2, axis=1)
    wk = jnp.concatenate([_pad_cols(w_uk[:, h * A_NOPE:(h + 1) * A_NOPE], LANES)
                          for h in range(A_HEADS)], axis=1)
    w_ki = w_in[:, O_KI:O_WI]
    w2 = jnp.concatenate([w_in[:, O_QB:O_KI],
                          w_ki, z(LANES - IDX_DIM), z(LANES - IDX_DIM), w_ki,
                          w_in[:, O_WI:O_GATE], z(LANES - IDX_HEADS),
                          w_in[:, O_GATE:]], axis=1)
    bf = lambda a: a.astype(BF16)
    wv_t = w_in[:, O_VC:O_QI].T
    return bf(w1), bf(wq1), bf(wq2), bf(wk), bf(w_uv), bf(w2), bf(wv_t), bf(w_out)


def _rope_tables(s_len):
    pos = jnp.arange(s_len, dtype=F32)
    freqs = ROPE_THETA ** (-jnp.arange(0, A_ROPE, 2, dtype=F32) / A_ROPE)
    ang = pos[:, None] * freqs[None, :]
    cos, sin = jnp.cos(ang), jnp.sin(ang)
    ones = jnp.ones((s_len, A_NOPE), F32)
    zeros = jnp.zeros((s_len, LANES - A_NOPE - A_ROPE), F32)
    ct = jnp.concatenate([ones, cos, cos, zeros], axis=1)
    st = jnp.concatenate([jnp.zeros((s_len, A_NOPE), F32), -sin, sin, zeros], axis=1)
    return ct, st


def kernel(x, c, w_ada, b_ada, w_in, q_norm_g, kv_norm_g, w_uq, w_uk, w_uv, w_out, ln_g, ln_b):
    bsz, s_len, d = x.shape
    depth = w_ada.shape[0]
    alpha = (2 * depth) ** 0.25
    slopes = 2.0 ** (-8.0 * jnp.arange(1, N_ALIBI + 1, dtype=F32) / N_ALIBI)
    slopes_b, slopes_c = slopes[:B_HEADS], slopes[B_HEADS:]
    ct, st = _rope_tables(s_len)
    mod_all = _mod_call(c, w_ada, b_ada)

    for l in range(depth):
        w1, wq1, wq2, wk, wv, w2, wv_t, wo = _layer_weights(w_in[l], w_uq[l], w_uk[l], w_uv[l], w_out[l])
        mod = mod_all[l].reshape(bsz, 1, 3 * d)
        qa, ka, va = _proj_mla_call(x, mod, w1, q_norm_g[l].reshape(1, -1), kv_norm_g[l].reshape(1, -1),
                                    wq1, wq2, wk, wv, ct, st)
        qb, kb, vb, qc, kc, vc_t, qi, ki, wi, gate = _proj_rest_call(x, mod, w2, wv_t)
        o_a = _mla_call(qa, ka, va)
        o_b = _dil_call(qb, kb, vb, slopes_b)
        o_c = _dsa_call(qi, ki, wi, qc, kc, vc_t, slopes_c)
        x = _out_call(alpha, x, mod, o_a, o_b, o_c, gate, wo,
                      ln_g[l].reshape(1, -1), ln_b[l].reshape(1, -1))
    return x
```

```python
import functools

import numpy as np
import jax
import jax.numpy as jnp
from jax import lax
from jax.experimental import pallas as pl
from jax.experimental.pallas import tpu as pltpu

F32 = jnp.float32
BF16 = jnp.bfloat16
I32 = jnp.int32

D_MODEL = 1024
HEAD_DIM = 64
A_HEADS = 6
A_NOPE = 64
A_ROPE = 32
A_V = 64
Q_LORA = 768
KV_LORA = 256
ROPE_THETA = 10000.0
B_HEADS = 6
DILATED_PATTERNS = ((128, 1), (512, 4), (2048, 16))
C_HEADS = 4
IDX_HEADS = 8
IDX_DIM = 64
TOPK_MAX = 256
A_WIDTH = A_HEADS * A_V
B_WIDTH = B_HEADS * HEAD_DIM
C_WIDTH = C_HEADS * HEAD_DIM
MIX_WIDTH = A_WIDTH + B_WIDTH + C_WIDTH
N_ALIBI = B_HEADS + C_HEADS
SPLIT_SIZES = (Q_LORA, KV_LORA, A_ROPE, B_WIDTH, B_WIDTH, B_WIDTH, C_WIDTH, C_WIDTH, C_WIDTH,
               IDX_HEADS * IDX_DIM, IDX_DIM, IDX_HEADS, MIX_WIDTH)
_OFFS = [0] + [int(v) for v in np.cumsum(SPLIT_SIZES)]
(O_CQ, O_CKV, O_KR, O_QB, O_KB, O_VB, O_QC, O_KC, O_VC, O_QI, O_KI, O_WI, O_GATE, _O_END) = _OFFS

LANES = 128
SUBLANES = 8
LOG2E = 1.4426950408889634
MASKED = -1e30
VMEM_LIMIT = 48 * 1024 * 1024

PROJ_ROWS = 512
MLA_TQ = 1024
DIL_N = 128
DIL_UNROLL = 16
DIL_MIX_ROWS = 512
DSA_Q = 256
DSA_ONES = 16
DSA_CH = 512
FLT_MAX = float(np.finfo(np.float32).max)
_KEY_NEG_MAX = int(np.int32(np.uint32(0xFF7FFFFF) ^ np.uint32(0x7FFFFFFF)))
_KEY_FIRST_NAN = 0x7F800001


def _dot(a, b):
    return jnp.dot(a, b, preferred_element_type=F32)


def _dot_nt(a, b):
    return lax.dot_general(a, b, (((1,), (1,)), ((), ())), preferred_element_type=F32)


def _cparams(sem):
    return pltpu.CompilerParams(dimension_semantics=sem, vmem_limit_bytes=VMEM_LIMIT)


def _mod_kernel(c_ref, w_ref, b_ref, o_ref):
    c = c_ref[...]
    sc = (c * jax.nn.sigmoid(c)).astype(BF16)
    o_ref[0] = _dot(sc, w_ref[0].astype(BF16)) + b_ref[0]


def _mod_call(c, w_ada, b_ada):
    depth, d, n3 = w_ada.shape
    bsz = c.shape[0]
    tn = 1024
    return pl.pallas_call(
        _mod_kernel,
        grid=(depth, n3 // tn),
        in_specs=[pl.BlockSpec((bsz, d), lambda l, j: (0, 0)),
                  pl.BlockSpec((1, d, tn), lambda l, j: (l, 0, j)),
                  pl.BlockSpec((1, 1, tn), lambda l, j: (l, 0, j))],
        out_specs=pl.BlockSpec((1, bsz, tn), lambda l, j: (l, 0, j)),
        out_shape=jax.ShapeDtypeStruct((depth, bsz, n3), F32),
        compiler_params=_cparams(("arbitrary", "arbitrary")),
    )(c, w_ada, b_ada.reshape(depth, 1, n3))


def _modulate(x_ref, mod_ref):
    mod = mod_ref[...]
    shift = mod[:, :D_MODEL]
    scale = mod[:, D_MODEL:2 * D_MODEL]
    return (x_ref[...] * (1.0 + scale) + shift).astype(BF16)


def _rms(x, g, eps=1e-6):
    ms = jnp.mean(x * x, axis=-1, keepdims=True)
    return (x * lax.rsqrt(ms + eps) * g).astype(BF16)


def _proj_mla_kernel(x_ref, mod_ref, w1_ref, gq_ref, gkv_ref, wq1_ref, wq2_ref, wk_ref, wv_ref,
                     ct_ref, st_ref, q_ref, k_ref, v_ref):
    h = _modulate(x_ref, mod_ref)
    pa = _dot(h, w1_ref[...])
    cq = _rms(pa[:, :Q_LORA], gq_ref[...])
    ckv = _rms(pa[:, Q_LORA:Q_LORA + KV_LORA], gkv_ref[...])
    ct = ct_ref[...]
    st = st_ref[...]
    ct6 = jnp.concatenate([ct] * A_HEADS, axis=1)
    st6 = jnp.concatenate([st] * A_HEADS, axis=1)
    qscale = (A_NOPE + A_ROPE) ** -0.5 * LOG2E
    q = (_dot(cq, wq1_ref[...]) * ct6 + _dot(cq, wq2_ref[...]) * st6) * qscale
    q_ref[...] = q.astype(BF16)
    o = Q_LORA + KV_LORA
    kr = pa[:, o:o + LANES] * ct + pa[:, o + LANES:o + 2 * LANES] * st
    k = _dot(ckv, wk_ref[...]) + jnp.concatenate([kr] * A_HEADS, axis=1)
    k_ref[...] = k.astype(BF16)
    v_ref[...] = _dot(ckv, wv_ref[...]).astype(BF16)


def _proj_mla_call(x, mod, w1, gq, gkv, wq1, wq2, wk, wv, ct, st):
    bsz, s_len, d = x.shape
    tm = PROJ_ROWS
    hw = A_HEADS * LANES
    full = lambda a: pl.BlockSpec(a.shape, lambda b, i: (0,) * a.ndim)
    return pl.pallas_call(
        _proj_mla_kernel,
        grid=(bsz, s_len // tm),
        in_specs=[pl.BlockSpec((None, tm, d), lambda b, i: (b, i, 0)),
                  pl.BlockSpec((None, 1, 3 * d), lambda b, i: (b, 0, 0)),
                  full(w1), full(gq), full(gkv), full(wq1), full(wq2), full(wk), full(wv),
                  pl.BlockSpec((tm, LANES), lambda b, i: (i, 0)),
                  pl.BlockSpec((tm, LANES), lambda b, i: (i, 0))],
        out_specs=[pl.BlockSpec((None, tm, hw), lambda b, i: (b, i, 0)),
                   pl.BlockSpec((None, tm, hw), lambda b, i: (b, i, 0)),
                   pl.BlockSpec((None, tm, A_WIDTH), lambda b, i: (b, i, 0))],
        out_shape=[jax.ShapeDtypeStruct((bsz, s_len, hw), BF16),
                   jax.ShapeDtypeStruct((bsz, s_len, hw), BF16),
                   jax.ShapeDtypeStruct((bsz, s_len, A_WIDTH), BF16)],
        compiler_params=_cparams(("parallel", "parallel")),
    )(x, mod, w1, gq, gkv, wq1, wq2, wk, wv, ct, st)


_R_QB, _R_KB, _R_VB = 0, B_WIDTH, 2 * B_WIDTH
_R_QC = 3 * B_WIDTH
_R_KC, _R_VC = _R_QC + C_WIDTH, _R_QC + 2 * C_WIDTH
_R_QI = _R_QC + 3 * C_WIDTH
_R_KI = _R_QI + IDX_HEADS * IDX_DIM
_R_WI = _R_KI + 2 * LANES
_R_GATE = _R_WI + LANES
_R_END = _R_GATE + MIX_WIDTH


def _proj_rest_kernel(x_ref, mod_ref, w2_ref, wvt_ref, qb_ref, kb_ref, vb_ref, qc_ref, kc_ref, vct_ref,
                      qi_ref, ki_ref, wi_ref, gate_ref):
    h = _modulate(x_ref, mod_ref)

    def proj(start, width):
        return _dot(h, w2_ref[:, start:start + width])

    inv_sqrt_dh = HEAD_DIM ** -0.5 * LOG2E
    for ref, val in ((qb_ref, proj(_R_QB, B_WIDTH) * inv_sqrt_dh), (kb_ref, proj(_R_KB, B_WIDTH)),
                     (vb_ref, proj(_R_VB, B_WIDTH))):
        for p in range(B_HEADS // 2):
            ref[p] = val[:, p * LANES:(p + 1) * LANES]
    qc_ref[...] = (proj(_R_QC, C_WIDTH) * inv_sqrt_dh).astype(BF16)
    kc_ref[...] = proj(_R_KC, C_WIDTH).astype(BF16)
    vct_ref[...] = _dot_nt(wvt_ref[...], h).astype(BF16)
    qi_ref[...] = (proj(_R_QI, IDX_HEADS * IDX_DIM) * IDX_DIM ** -0.5).astype(BF16)
    ki_ref[...] = proj(_R_KI, 2 * LANES).astype(BF16)
    wi_ref[...] = proj(_R_WI, LANES) * IDX_HEADS ** -0.5
    gate_ref[...] = proj(_R_GATE, MIX_WIDTH)


def _proj_rest_call(x, mod, w2, wv_t):
    bsz, s_len, d = x.shape
    tm = PROJ_ROWS
    widths = (B_WIDTH, B_WIDTH, B_WIDTH, C_WIDTH, C_WIDTH, C_WIDTH, IDX_HEADS * IDX_DIM,
              2 * LANES, LANES, MIX_WIDTH)
    dtypes = (F32,) * 3 + (BF16,) * 5 + (F32, F32)
    slab = pl.BlockSpec((None, B_HEADS // 2, tm, LANES), lambda b, i: (b, 0, i, 0))
    slab_shape = (bsz, B_HEADS // 2, s_len, LANES)
    return pl.pallas_call(
        _proj_rest_kernel,
        grid=(bsz, s_len // tm),
        in_specs=[pl.BlockSpec((None, tm, d), lambda b, i: (b, i, 0)),
                  pl.BlockSpec((None, 1, 3 * d), lambda b, i: (b, 0, 0)),
                  pl.BlockSpec(w2.shape, lambda b, i: (0, 0)),
                  pl.BlockSpec(wv_t.shape, lambda b, i: (0, 0))],
        out_specs=[slab if n < 3 else pl.BlockSpec((None, C_WIDTH, tm), lambda b, i: (b, 0, i)) if n == 5 else
                   pl.BlockSpec((None, tm, w), lambda b, i: (b, i, 0)) for n, w in enumerate(widths)],
        out_shape=[jax.ShapeDtypeStruct(slab_shape if n < 3 else (bsz, C_WIDTH, s_len) if n == 5 else
                                        (bsz, s_len, w), dt) for n, (w, dt) in enumerate(zip(widths, dtypes))],
        compiler_params=_cparams(("parallel", "parallel")),
    )(x, mod, w2, wv_t)


def _with_ones(v):
    return jnp.concatenate([v, jnp.ones_like(v)], axis=1)


def _init_state(m_ref, acc_ref):
    m_ref[...] = jnp.full(m_ref.shape, MASKED, F32)
    acc_ref[...] = jnp.zeros(acc_ref.shape, F32)


def _pair_output(acc_ref, i0, i1):
    lane = lax.broadcasted_iota(I32, (acc_ref.shape[1], LANES), 1)
    a0, a1 = acc_ref[i0], acc_ref[i1]
    o0 = a0[:, :LANES] / a0[:, LANES:]
    o1 = a1[:, :LANES] / a1[:, LANES:]
    return jnp.where(lane < HEAD_DIM, o0, o1)


def _mla_kernel(q_ref, k_ref, v_ref, o_ref, m_ref, acc_ref):
    qi = pl.program_id(2)
    tq = q_ref.shape[0]
    _init_state(m_ref, acc_ref)
    lane = lax.broadcasted_iota(I32, (tq, 2 * LANES), 1)
    row = lax.broadcasted_iota(I32, (tq, tq), 0)
    col = lax.broadcasted_iota(I32, (tq, tq), 1)

    def step(c, diagonal):
        ks = pl.ds(pl.multiple_of(c * tq, tq), tq)
        k = k_ref[ks, :]
        zero = jnp.zeros_like(k)
        k2 = jnp.concatenate([jnp.where(lane < LANES, k, zero), jnp.where(lane >= LANES, k, zero)], axis=0)
        s2 = _dot_nt(q_ref[...], k2)
        ps, alphas = [], []
        for hh in range(2):
            s = s2[:, hh * tq:(hh + 1) * tq]
            if diagonal:
                s = jnp.where(col <= row, s, MASKED)
            m_prev = m_ref[hh]
            m_new = jnp.maximum(m_prev, jnp.max(s, axis=1, keepdims=True))
            alphas.append(jnp.exp2(m_prev - m_new))
            ps.append(jnp.exp2(s - jnp.tile(m_new, (1, tq // LANES))).astype(BF16))
            m_ref[hh] = m_new
        pv = _dot(jnp.concatenate(ps, axis=0), _with_ones(v_ref[ks, :]))
        for hh in range(2):
            acc_ref[hh] = jnp.tile(alphas[hh], (1, 2)) * acc_ref[hh] + pv[hh * tq:(hh + 1) * tq]

    def body(c, carry):
        step(c, False)
        return carry

    lax.fori_loop(0, qi, body, 0)
    step(qi, True)
    o_ref[...] = _pair_output(acc_ref, 0, 1)


def _mla_call(q, k, v):
    bsz, s_len, _ = q.shape
    tq = MLA_TQ
    npair = A_HEADS // 2
    return pl.pallas_call(
        _mla_kernel,
        grid=(bsz, npair, s_len // tq),
        in_specs=[pl.BlockSpec((None, tq, 2 * LANES), lambda b, p, i: (b, i, p)),
                  pl.BlockSpec((None, s_len, 2 * LANES), lambda b, p, i: (b, 0, p)),
                  pl.BlockSpec((None, s_len, LANES), lambda b, p, i: (b, 0, p))],
        out_specs=pl.BlockSpec((None, tq, LANES), lambda b, p, i: (b, i, p)),
        out_shape=jax.ShapeDtypeStruct((bsz, s_len, A_WIDTH), F32),
        scratch_shapes=[pltpu.VMEM((2, tq, LANES), F32), pltpu.VMEM((2, tq, 2 * LANES), F32)],
        compiler_params=_cparams(("parallel", "parallel", "arbitrary")),
    )(q, k, v)


def _dil_kernel(q_ref, k_ref, v_ref, bias_ref, o_ref, oacc_ref, lse_ref):
    n = DIL_N
    s_len = q_ref.shape[0]
    lane = lax.broadcasted_iota(I32, (2 * n, LANES), 1)
    first = lax.broadcasted_iota(I32, (n, LANES), 1) < HEAD_DIM

    for pat, (_, dil) in enumerate(DILATED_PATTERNS):
        nb = s_len // dil // n
        stride = dil if dil > 1 else None

        def unit(t, pat=pat, dil=dil, nb=nb, stride=stride):
            r, j = t // nb, t % nb
            q_start = r + j * (n * dil)
            b_start = r + jnp.maximum(j - 1, 0) * (n * dil)
            table = jnp.minimum(j, 1)
            q = q_ref[pl.ds(q_start, n, stride=stride), :].astype(BF16)
            k = k_ref[pl.ds(b_start, 2 * n, stride=stride), :].astype(BF16)
            v = _with_ones(v_ref[pl.ds(b_start, 2 * n, stride=stride), :].astype(BF16))
            zero = jnp.zeros_like(k)
            k2 = jnp.concatenate([jnp.where(lane < HEAD_DIM, k, zero), jnp.where(lane >= HEAD_DIM, k, zero)],
                                 axis=0)
            s2 = _dot_nt(q, k2)
            ps, ms = [], []
            for hh in range(2):
                s = s2[:, hh * 2 * n:(hh + 1) * 2 * n] + bias_ref[pat, table, hh]
                m = jnp.max(s, axis=1, keepdims=True)
                ps.append(jnp.exp2(s - m).astype(BF16))
                ms.append(m)
            pv = _dot(jnp.concatenate(ps, axis=0), v)
            outs = [pv[hh * n:(hh + 1) * n, :LANES] / pv[hh * n:(hh + 1) * n, LANES:] for hh in range(2)]
            lses = [ms[hh] + jnp.log2(pv[hh * n:(hh + 1) * n, LANES:]) for hh in range(2)]
            oacc_ref[pat, pl.ds(q_start, n, stride=stride), :] = jnp.where(first, outs[0], outs[1])
            lse_ref[pat, pl.ds(q_start, n, stride=stride), :] = jnp.where(first, lses[0], lses[1])

        def body(i, carry, unit=unit):
            for u in range(DIL_UNROLL):
                unit(i * DIL_UNROLL + u)
            return carry

        lax.fori_loop(0, s_len // n // DIL_UNROLL, body, 0)

    def mix(i, carry):
        rs = pl.ds(pl.multiple_of(i * DIL_MIX_ROWS, DIL_MIX_ROWS), DIL_MIX_ROWS)
        ls = [lse_ref[p, rs, :] for p in range(len(DILATED_PATTERNS))]
        mx = functools.reduce(jnp.maximum, ls)
        es = [jnp.exp2(l - mx) for l in ls]
        num = functools.reduce(lambda a, b: a + b, [e * oacc_ref[p, rs, :] for p, e in enumerate(es)])
        o_ref[rs, :] = num / functools.reduce(lambda a, b: a + b, es)
        return carry

    lax.fori_loop(0, s_len // DIL_MIX_ROWS, mix, 0)


def _dil_bias(slopes_b):
    n = DIL_N
    kj = jnp.arange(2 * n)
    step = jnp.arange(n)[:, None] + n - kj[None, :]
    valid = (step >= 0) & (step <= n)
    tables = []
    for _, dil in DILATED_PATTERNS:
        bias = -(slopes_b * LOG2E)[:, None, None] * (step * dil).astype(F32)[None]
        banded = jnp.where(valid[None], bias, MASKED)
        first = jnp.concatenate([banded[:, :, n:], jnp.full_like(banded[:, :, n:], MASKED)], axis=2)
        tables.append(jnp.stack([first, banded]))
    return jnp.stack(tables).reshape(len(DILATED_PATTERNS), 2, B_HEADS // 2, 2, n, 2 * n)


def _dil_call(q, k, v, slopes_b):
    bsz, npair, s_len, _ = q.shape
    npat = len(DILATED_PATTERNS)
    for window, dil in DILATED_PATTERNS:
        assert window // dil == DIL_N and s_len % (dil * DIL_N * 2) == 0
    bias = _dil_bias(slopes_b)
    slab = pl.BlockSpec((None, None, s_len, LANES), lambda b, p: (b, p, 0, 0))
    return pl.pallas_call(
        _dil_kernel,
        grid=(bsz, npair),
        in_specs=[slab, slab, slab,
                  pl.BlockSpec((npat, 2, None, 2, DIL_N, 2 * DIL_N), lambda b, p: (0, 0, p, 0, 0, 0))],
        out_specs=pl.BlockSpec((None, s_len, LANES), lambda b, p: (b, 0, p)),
        out_shape=jax.ShapeDtypeStruct((bsz, s_len, npair * LANES), F32),
        scratch_shapes=[pltpu.VMEM((npat, s_len, LANES), F32), pltpu.VMEM((npat, s_len, LANES), F32)],
        compiler_params=_cparams(("parallel", "parallel")),
    )(q, k, v, bias)


def _sortable(x):
    b = lax.bitcast_convert_type(x, I32)
    return b ^ ((b >> 31) & 0x7FFFFFFF)


def _unsortable(k):
    return lax.bitcast_convert_type(k ^ ((k >> 31) & 0x7FFFFFFF), F32)


def _dsa_kernel(qi_ref, ki_ref, wi_ref, qc_ref, kc_ref, vt_ref, tri_ref, ztab_ref, wtab_ref, slope_ref, o_ref,
                sc_ref, sel_ref, ist_ref, fst_ref, m_ref, acc_ref):
    qb = pl.program_id(1)
    nq, ch = DSA_Q, DSA_CH
    g = ch // SUBLANES
    q0 = qb * nq
    nk = (q0 + nq + ch - 1) // ch
    key_i = lax.broadcasted_iota(I32, (ch, nq), 0)
    qry_i = lax.broadcasted_iota(I32, (ch, nq), 1)
    rel = key_i - qry_i

    def rows(a):
        return jnp.broadcast_to(a[:1], (SUBLANES, nq))

    def groups(a):
        return a.reshape(g, SUBLANES, nq)

    def gsum(a):
        return jnp.sum(jnp.sum(a.reshape(SUBLANES, g // SUBLANES, SUBLANES, nq), axis=1), axis=0)

    def gmax(a):
        return jnp.max(jnp.max(a.reshape(SUBLANES, g // SUBLANES, SUBLANES, nq), axis=1), axis=0)

    def gmin(a):
        return jnp.min(jnp.min(a.reshape(SUBLANES, g // SUBLANES, SUBLANES, nq), axis=1), axis=0)

    def colsum(a):
        return rows(jnp.sum(gsum(groups(a)), axis=0, keepdims=True))

    w_t = wi_ref[...].T
    w_rows = [w_t[h:h + 1, :] for h in range(IDX_HEADS)]

    def score_body(c, carry):
        s1, s2 = carry
        ks = pl.ds(pl.multiple_of(c * ch, ch), ch)
        k_lo = ki_ref[ks, :LANES]
        k_hi = ki_ref[ks, LANES:]
        sc = jnp.zeros((ch, nq), F32)
        for j in range(IDX_HEADS // 2):
            qj = qi_ref[:, j * LANES:(j + 1) * LANES]
            sc = sc + w_rows[2 * j] * jnp.maximum(_dot_nt(k_lo, qj), 0.0)
            sc = sc + w_rows[2 * j + 1] * jnp.maximum(_dot_nt(k_hi, qj), 0.0)
        causal = rel <= q0 - c * ch
        sc_ref[c] = groups(jnp.where(causal, sc, -jnp.inf))
        kept = jnp.where(causal, sc, 0.0)
        return s1 + colsum(kept), s2 + colsum(kept * kept)

    zeros = jnp.zeros((SUBLANES, nq), F32)
    s1, s2 = lax.fori_loop(0, nk, score_body, (zeros, zeros))

    def count(*preds):
        def body(c, accs):
            s = sc_ref[c]
            return tuple(a + gsum(jnp.where(p(s), 1.0, 0.0)) for a, p in zip(accs, preds))
        accs = lax.fori_loop(0, nk, body, (zeros,) * len(preds))
        return tuple(rows(jnp.sum(a, axis=0, keepdims=True)) for a in accs)

    topk = float(TOPK_MAX)
    target = topk + 0.5
    T_KEY, LO, HI = 0, 1, 2
    C_LO, C_HI, REAL_LO, REAL_HI, SIDE, THR, C_THR, DONE, STEP, NEAR = range(10)
    fst_ref[THR] = jnp.full((SUBLANES, nq), -FLT_MAX, F32)
    fst_ref[C_THR] = zeros

    @pl.when(q0 + 1 > TOPK_MAX)
    def _():
        n = (q0 + 1 + lax.broadcasted_iota(I32, (SUBLANES, nq), 1)).astype(F32)
        mu = s1 / n
        sigma = jnp.sqrt(jnp.maximum(s2 / n - mu * mu, 0.0))
        fst_ref[STEP] = wtab_ref[...] * sigma * 1.5
        c_zero, c_pos = count(lambda s: s >= 0.0, lambda s: s > 0.0)
        zero_tie = (c_pos < topk) & (c_zero >= topk)
        above = c_pos >= topk
        below = c_zero < topk
        key0 = jnp.zeros((SUBLANES, nq), I32)
        lo0 = jnp.where(above, key0, jnp.full((SUBLANES, nq), _KEY_NEG_MAX, I32))
        hi0 = jnp.where(below, key0, jnp.full((SUBLANES, nq), _KEY_FIRST_NAN, I32))
        ist_ref[LO] = lo0
        ist_ref[HI] = hi0
        ist_ref[T_KEY] = jnp.clip(_sortable(mu + ztab_ref[...] * sigma), lo0 + 1, hi0 - 1)
        fst_ref[C_LO] = jnp.where(above, c_zero, n)
        fst_ref[C_HI] = jnp.where(below, c_zero, 0.0)
        fst_ref[REAL_LO] = jnp.where(above, 1.0, 0.0)
        fst_ref[REAL_HI] = jnp.where(below, 1.0, 0.0)
        fst_ref[SIDE] = zeros
        fst_ref[NEAR] = zeros
        fst_ref[THR] = jnp.where(zero_tie, 0.0, -FLT_MAX)
        fst_ref[C_THR] = jnp.where(zero_tie, c_zero, 0.0)
        done0 = jnp.where(zero_tie, 1.0, 0.0)
        fst_ref[DONE] = done0

        def cond(st):
            return jnp.logical_and(st[0] < 60, st[1] > 0.0)

        def body(st):
            it = st[0]
            t_key, lo, hi = ist_ref[T_KEY], ist_ref[LO], ist_ref[HI]
            c_lo, c_hi, done, side = fst_ref[C_LO], fst_ref[C_HI], fst_ref[DONE], fst_ref[SIDE]
            t = _unsortable(t_key)
            (c,) = count(lambda s: s >= t[None])
            live = done == 0.0
            hit = live & (jnp.abs(c - topk) <= 1.0)
            up = live & (c > topk + 1.0)
            dn = live & (c < topk - 1.0)
            moved = jnp.where(up, 1.0, jnp.where(dn, -1.0, 0.0))
            repeat = (moved == side) & (moved != 0.0)
            lo = jnp.where(up, t_key, lo)
            c_lo = jnp.where(up, c, c_lo)
            real_lo = jnp.where(up, 1.0, fst_ref[REAL_LO])
            hi = jnp.where(dn, t_key, hi)
            c_hi = jnp.where(dn, c, c_hi)
            real_hi = jnp.where(dn, 1.0, fst_ref[REAL_HI])
            gap = lax.shift_right_logical(hi - lo, 1)
            shut = (gap == 0) & live & jnp.logical_not(hit)
            fst_ref[THR] = jnp.where(hit, t, jnp.where(shut, _unsortable(lo), fst_ref[THR]))
            fst_ref[C_THR] = jnp.where(hit, c, jnp.where(shut, c_lo, fst_ref[C_THR]))
            fst_ref[NEAR] = jnp.where(hit, 1.0, fst_ref[NEAR])
            done = jnp.where(hit | shut, 1.0, done)
            lo_f, hi_f = _unsortable(lo), _unsortable(hi)
            both = real_lo * real_hi
            frac = jnp.where(repeat, 0.5, (c_lo - target) / (c_lo - c_hi))
            inside = lo_f + (hi_f - lo_f) * frac
            guess = jnp.where(both > 0.0, inside, t + fst_ref[STEP] * (c - target))
            g_key = jnp.clip(_sortable(guess), lo + 1, hi - 1)
            late = jnp.where(it >= 14, 1.0, 0.0)
            ist_ref[T_KEY] = jnp.where(late > 0.0, lo + gap, g_key)
            ist_ref[LO] = lo
            ist_ref[HI] = hi
            fst_ref[C_LO] = c_lo
            fst_ref[C_HI] = c_hi
            fst_ref[REAL_LO] = real_lo
            fst_ref[REAL_HI] = real_hi
            fst_ref[SIDE] = jnp.where(moved != 0.0, moved, side)
            fst_ref[DONE] = done
            return it + 1, float(nq) - jnp.sum(done[:1])

        lax.while_loop(cond, body, (jnp.int32(0), float(nq) - jnp.sum(done0[:1])))

        t, c_t, near = fst_ref[THR], fst_ref[C_THR], fst_ref[NEAR]
        short = (near > 0.0) & (c_t == topk - 1.0)
        extra = (near > 0.0) & (c_t == topk + 1.0)
        inf = jnp.full((SUBLANES, nq), jnp.inf, F32)

        def around(c, carry):
            lower, upper = carry
            s = sc_ref[c]
            below = s < t[None]
            return (jnp.maximum(lower, gmax(jnp.where(below, s, -jnp.inf))),
                    jnp.minimum(upper, gmin(jnp.where(below, jnp.inf, s))))

        lower, upper = lax.fori_loop(0, nk, around, (-inf, inf))
        lower = rows(jnp.max(lower, axis=0, keepdims=True))
        upper = rows(jnp.min(upper, axis=0, keepdims=True))
        x = jnp.where(short, lower, jnp.where(extra, upper, t))

        def recount(c, carry):
            n_ge, n_gt, nxt = carry
            s = sc_ref[c]
            gt = s > x[None]
            return (n_ge + gsum(jnp.where(s >= x[None], 1.0, 0.0)), n_gt + gsum(jnp.where(gt, 1.0, 0.0)),
                    jnp.minimum(nxt, gmin(jnp.where(gt, s, jnp.inf))))

        n_ge, n_gt, nxt = lax.fori_loop(0, nk, recount, (zeros, zeros, inf))
        n_ge = rows(jnp.sum(n_ge, axis=0, keepdims=True))
        n_gt = rows(jnp.sum(n_gt, axis=0, keepdims=True))
        nxt = rows(jnp.min(nxt, axis=0, keepdims=True))
        step_up = extra & (n_gt == topk)
        fst_ref[THR] = jnp.where(step_up, nxt, x)
        fst_ref[C_THR] = jnp.where(step_up, n_gt, n_ge)

    thr = fst_ref[THR]
    has_tie = jnp.max(fst_ref[C_THR]) > topk

    @pl.when(jnp.logical_not(has_tie))
    def _():
        def body(c, carry):
            sel_ref[c] = jnp.where(sc_ref[c] >= thr[None], 0.0, MASKED)
            return carry
        lax.fori_loop(0, nk, body, 0)

    @pl.when(has_tie)
    def _():
        (n_gt,) = count(lambda s: s > thr[None])
        need = topk - n_gt

        def body(c, seen):
            s = sc_ref[c]
            eq = s == thr[None]
            ones = jnp.where(eq, 1.0, 0.0).reshape(ch, nq).astype(BF16)
            rank = groups(_dot(tri_ref[...], ones)) + seen[None]
            take = (s > thr[None]) | (eq & (rank <= need[None]))
            sel_ref[c] = jnp.where(take, 0.0, MASKED)
            return rows(rank[g - 1, SUBLANES - 1:SUBLANES, :])
        lax.fori_loop(0, nk, body, zeros)

    m_ref[...] = jnp.full(m_ref.shape, MASKED, F32)
    acc_ref[...] = jnp.zeros(acc_ref.shape, F32)
    lane4 = lax.broadcasted_iota(I32, (ch, C_WIDTH), 1)
    relf = rel.astype(F32)
    ones_rows = jnp.ones((DSA_ONES, ch), BF16)
    ag = acc_ref.shape[1] // SUBLANES

    def attn_body(c, carry):
        ks = pl.ds(pl.multiple_of(c * ch, ch), ch)
        dist = relf + (c * ch - q0).astype(F32)
        sel = sel_ref[c].reshape(ch, nq)
        k = kc_ref[ks, :]
        zero = jnp.zeros_like(k)
        k4 = jnp.concatenate([jnp.where((lane4 >= h * HEAD_DIM) & (lane4 < (h + 1) * HEAD_DIM), k, zero)
                              for h in range(C_HEADS)], axis=0)
        s4 = _dot_nt(k4, qc_ref[...])
        prs, alphas = [], []
        for h in range(C_HEADS):
            s = groups(s4[h * ch:(h + 1) * ch] + (slope_ref[h] * dist + sel))
            m_prev = m_ref[h]
            m_new = jnp.maximum(m_prev, rows(jnp.max(gmax(s), axis=0, keepdims=True)))
            alphas.append(jnp.exp2(m_prev - m_new))
            prs.append(jnp.exp2(s - m_new[None]).reshape(ch, nq).astype(BF16))
            m_ref[h] = m_new
        for p in range(C_HEADS // 2):
            v_t = jnp.concatenate([vt_ref[p * LANES:(p + 1) * LANES, ks], ones_rows], axis=0)
            pv = _dot(v_t, jnp.concatenate(prs[2 * p:2 * p + 2], axis=1))
            for hh in range(2):
                h = 2 * p + hh
                acc = acc_ref[h].reshape(ag, SUBLANES, nq) * alphas[h][None]
                acc_ref[h] = acc.reshape(ag * SUBLANES, nq) + pv[:, hh * nq:(hh + 1) * nq]
        return carry

    lax.fori_loop(0, nk, attn_body, 0)
    dim = lax.broadcasted_iota(I32, (LANES, nq), 0)
    for p in range(C_HEADS // 2):
        outs = []
        for hh in range(2):
            acc = acc_ref[2 * p + hh]
            den = acc[LANES:LANES + SUBLANES]
            outs.append((acc[:LANES].reshape(LANES // SUBLANES, SUBLANES, nq) / den[None]).reshape(LANES, nq))
        o_ref[:, p * LANES:(p + 1) * LANES] = jnp.where(dim < HEAD_DIM, outs[0], outs[1]).T


def _quantile_table(s_len):
    n = jnp.arange(1, s_len + 1, dtype=F32)
    tail = jnp.clip((TOPK_MAX + 0.5) / n, 1e-6, 1.0 - 1e-6)
    z = jax.scipy.special.ndtri(1.0 - tail)
    pdf = jnp.exp(-0.5 * z * z) * (2.0 * np.pi) ** -0.5
    rep = lambda a: jnp.broadcast_to(a[None, :], (SUBLANES, s_len))
    return rep(z), rep(1.0 / (n * pdf))


def _dsa_call(qi, ki, wi, qc, kc, vc_t, slopes_c):
    bsz, s_len, _ = qc.shape
    nq, ch = DSA_Q, DSA_CH
    tri = (jnp.arange(ch)[:, None] >= jnp.arange(ch)[None, :]).astype(BF16)
    ztab, wtab = _quantile_table(s_len)
    blk = lambda w: pl.BlockSpec((None, nq, w), lambda b, i: (b, i, 0))
    seq = lambda w: pl.BlockSpec((None, s_len, w), lambda b, i: (b, 0, 0))
    tab = pl.BlockSpec((SUBLANES, nq), lambda b, i: (0, i))
    return pl.pallas_call(
        _dsa_kernel,
        grid=(bsz, s_len // nq),
        in_specs=[blk(IDX_HEADS * IDX_DIM), seq(2 * LANES), blk(LANES),
                  blk(C_WIDTH), seq(C_WIDTH),
                  pl.BlockSpec((None, C_WIDTH, s_len), lambda b, i: (b, 0, 0)),
                  pl.BlockSpec((ch, ch), lambda b, i: (0, 0)),
                  tab, tab,
                  pl.BlockSpec(memory_space=pltpu.SMEM)],
        out_specs=blk(C_WIDTH),
        out_shape=jax.ShapeDtypeStruct((bsz, s_len, C_WIDTH), F32),
        scratch_shapes=[pltpu.VMEM((s_len // ch, ch // SUBLANES, SUBLANES, nq), F32),
                        pltpu.VMEM((s_len // ch, ch // SUBLANES, SUBLANES, nq), F32),
                        pltpu.VMEM((3, SUBLANES, nq), I32),
                        pltpu.VMEM((10, SUBLANES, nq), F32),
                        pltpu.VMEM((C_HEADS, SUBLANES, nq), F32),
                        pltpu.VMEM((C_HEADS, LANES + DSA_ONES, nq), F32)],
        compiler_params=_cparams(("parallel", "arbitrary")),
    )(qi, ki, wi, qc, kc, vc_t, tri, ztab, wtab, slopes_c * LOG2E)


def _out_kernel(alpha, x_ref, mod_ref, oa_ref, ob_ref, oc_ref, gate_ref, wo_ref, g_ref, b_ref, y_ref):
    pg = gate_ref[...]
    sg = pg * jax.nn.sigmoid(pg)
    ya = (oa_ref[...] * sg[:, :A_WIDTH]).astype(BF16)
    yb = (ob_ref[...] * sg[:, A_WIDTH:A_WIDTH + B_WIDTH]).astype(BF16)
    yc = (oc_ref[...] * sg[:, A_WIDTH + B_WIDTH:]).astype(BF16)
    sub = (_dot(ya, wo_ref[:A_WIDTH, :]) + _dot(yb, wo_ref[A_WIDTH:A_WIDTH + B_WIDTH, :])
           + _dot(yc, wo_ref[A_WIDTH + B_WIDTH:, :]))
    gate = mod_ref[...][:, 2 * D_MODEL:]
    z = alpha * x_ref[...] + (1.0 + gate) * sub
    mu = jnp.mean(z, axis=-1, keepdims=True)
    zc = z - mu
    var = jnp.mean(zc * zc, axis=-1, keepdims=True)
    y_ref[...] = zc * lax.rsqrt(var + 1e-5) * g_ref[...] + b_ref[...]


def _out_call(alpha, x, mod, oa, ob, oc, gate, wo, g, b):
    bsz, s_len, d = x.shape
    tm = PROJ_ROWS
    rows = lambda w: pl.BlockSpec((None, tm, w), lambda bb, i: (bb, i, 0))
    full = lambda a: pl.BlockSpec(a.shape, lambda bb, i: (0,) * a.ndim)
    return pl.pallas_call(
        functools.partial(_out_kernel, alpha),
        grid=(bsz, s_len // tm),
        in_specs=[rows(d), pl.BlockSpec((None, 1, 3 * d), lambda bb, i: (bb, 0, 0)),
                  rows(A_WIDTH), rows(B_WIDTH), rows(C_WIDTH), rows(MIX_WIDTH),
                  full(wo), full(g), full(b)],
        out_specs=rows(d),
        out_shape=jax.ShapeDtypeStruct((bsz, s_len, d), F32),
        compiler_params=_cparams(("parallel", "parallel")),
    )(x, mod, oa, ob, oc, gate, wo, g, b)


def _pad_cols(a, width):
    return jnp.pad(a, ((0, 0), (0, width - a.shape[1])))


def _swap_halves(a):
    half = a.shape[1] // 2
    return jnp.concatenate([a[:, half:], a[:, :half]], axis=1)


def _layer_weights(w_in, w_uq, w_uk, w_uv, w_out):
    d = w_in.shape[0]
    z = lambda n: jnp.zeros((d, n), w_in.dtype)
    w_kr = w_in[:, O_KR:O_QB]
    w1 = jnp.concatenate([w_in[:, O_CQ:O_KR],
                          z(A_NOPE), w_kr, z(LANES - A_NOPE - A_ROPE),
                          z(A_NOPE), _swap_halves(w_kr), z(LANES - A_NOPE - A_ROPE)], axis=1)
    hq = A_NOPE + A_ROPE
    q1, q2 = [], []
    for h in range(A_HEADS):
        blk = w_uq[:, h * hq:(h + 1) * hq]
        q1.append(_pad_cols(blk, LANES))
        zq = jnp.zeros((w_uq.shape[0], A_NOPE), w_uq.dtype)
        q2.append(_pad_cols(jnp.concatenate([zq, _swap_halves(blk[:, A_NOPE:])], axis=1), LANES))
    wq1 = jnp.concatenate(q1, axis=1)
    wq2 = jnp.concatenate(q2, axis=1)
    wk = jnp.concatenate([_pad_cols(w_uk[:, h * A_NOPE:(h + 1) * A_NOPE], LANES)
                          for h in range(A_HEADS)], axis=1)
    w_ki = w_in[:, O_KI:O_WI]
    w2 = jnp.concatenate([w_in[:, O_QB:O_KI],
                          w_ki, z(LANES - IDX_DIM), z(LANES - IDX_DIM), w_ki,
                          w_in[:, O_WI:O_GATE], z(LANES - IDX_HEADS),
                          w_in[:, O_GATE:]], axis=1)
    bf = lambda a: a.astype(BF16)
    wv_t = w_in[:, O_VC:O_QI].T
    return bf(w1), bf(wq1), bf(wq2), bf(wk), bf(w_uv), bf(w2), bf(wv_t), bf(w_out)


def _rope_tables(s_len):
    pos = jnp.arange(s_len, dtype=F32)
    freqs = ROPE_THETA ** (-jnp.arange(0, A_ROPE, 2, dtype=F32) / A_ROPE)
    ang = pos[:, None] * freqs[None, :]
    cos, sin = jnp.cos(ang), jnp.sin(ang)
    ones = jnp.ones((s_len, A_NOPE), F32)
    zeros = jnp.zeros((s_len, LANES - A_NOPE - A_ROPE), F32)
    ct = jnp.concatenate([ones, cos, cos, zeros], axis=1)
    st = jnp.concatenate([jnp.zeros((s_len, A_NOPE), F32), -sin, sin, zeros], axis=1)
    return ct, st


def kernel(x, c, w_ada, b_ada, w_in, q_norm_g, kv_norm_g, w_uq, w_uk, w_uv, w_out, ln_g, ln_b):
    bsz, s_len, d = x.shape
    depth = w_ada.shape[0]
    alpha = (2 * depth) ** 0.25
    slopes = 2.0 ** (-8.0 * jnp.arange(1, N_ALIBI + 1, dtype=F32) / N_ALIBI)
    slopes_b, slopes_c = slopes[:B_HEADS], slopes[B_HEADS:]
    ct, st = _rope_tables(s_len)
    mod_all = _mod_call(c, w_ada, b_ada)

    for l in range(depth):
        w1, wq1, wq2, wk, wv, w2, wv_t, wo = _layer_weights(w_in[l], w_uq[l], w_uk[l], w_uv[l], w_out[l])
        mod = mod_all[l].reshape(bsz, 1, 3 * d)
        qa, ka, va = _proj_mla_call(x, mod, w1, q_norm_g[l].reshape(1, -1), kv_norm_g[l].reshape(1, -1),
                                    wq1, wq2, wk, wv, ct, st)
        qb, kb, vb, qc, kc, vc_t, qi, ki, wi, gate = _proj_rest_call(x, mod, w2, wv_t)
        o_a = _mla_call(qa, ka, va)
        o_b = _dil_call(qb, kb, vb, slopes_b)
        o_c = _dsa_call(qi, ki, wi, qc, kc, vc_t, slopes_c)
        x = _out_call(alpha, x, mod, o_a, o_b, o_c, gate, wo,
                      ln_g[l].reshape(1, -1), ln_b[l].reshape(1, -1))
    return x
```

```python
import functools

import numpy as np
import jax
import jax.numpy as jnp
from jax import lax
from jax.experimental import pallas as pl
from jax.experimental.pallas import tpu as pltpu

F32 = jnp.float32
BF16 = jnp.bfloat16
I32 = jnp.int32

D_MODEL = 1024
HEAD_DIM = 64
A_HEADS = 6
A_NOPE = 64
A_ROPE = 32
A_V = 64
Q_LORA = 768
KV_LORA = 256
ROPE_THETA = 10000.0
B_HEADS = 6
DILATED_PATTERNS = ((128, 1), (512, 4), (2048, 16))
C_HEADS = 4
IDX_HEADS = 8
IDX_DIM = 64
TOPK_MAX = 256
A_WIDTH = A_HEADS * A_V
B_WIDTH = B_HEADS * HEAD_DIM
C_WIDTH = C_HEADS * HEAD_DIM
MIX_WIDTH = A_WIDTH + B_WIDTH + C_WIDTH
N_ALIBI = B_HEADS + C_HEADS
SPLIT_SIZES = (Q_LORA, KV_LORA, A_ROPE, B_WIDTH, B_WIDTH, B_WIDTH, C_WIDTH, C_WIDTH, C_WIDTH,
               IDX_HEADS * IDX_DIM, IDX_DIM, IDX_HEADS, MIX_WIDTH)
_OFFS = [0] + [int(v) for v in np.cumsum(SPLIT_SIZES)]
(O_CQ, O_CKV, O_KR, O_QB, O_KB, O_VB, O_QC, O_KC, O_VC, O_QI, O_KI, O_WI, O_GATE, _O_END) = _OFFS

LANES = 128
SUBLANES = 8
LOG2E = 1.4426950408889634
MASKED = -1e30
VMEM_LIMIT = 48 * 1024 * 1024

PROJ_ROWS = 512
MLA_TQ = 1024
DIL_N = 128
DIL_UNROLL = 16
DIL_MIX_ROWS = 512
DSA_Q = 256
DSA_ONES = 16
DSA_CH = 512
FLT_MAX = float(np.finfo(np.float32).max)
_KEY_NEG_MAX = int(np.int32(np.uint32(0xFF7FFFFF) ^ np.uint32(0x7FFFFFFF)))
_KEY_FIRST_NAN = 0x7F800001


def _dot(a, b):
    return jnp.dot(a, b, preferred_element_type=F32)


def _dot_nt(a, b):
    return lax.dot_general(a, b, (((1,), (1,)), ((), ())), preferred_element_type=F32)


def _cparams(sem):
    return pltpu.CompilerParams(dimension_semantics=sem, vmem_limit_bytes=VMEM_LIMIT)


def _mod_kernel(c_ref, w_ref, b_ref, o_ref):
    c = c_ref[...]
    sc = (c * jax.nn.sigmoid(c)).astype(BF16)
    o_ref[0] = _dot(sc, w_ref[0].astype(BF16)) + b_ref[0]


def _mod_call(c, w_ada, b_ada):
    depth, d, n3 = w_ada.shape
    bsz = c.shape[0]
    tn = 1024
    return pl.pallas_call(
        _mod_kernel,
        grid=(depth, n3 // tn),
        in_specs=[pl.BlockSpec((bsz, d), lambda l, j: (0, 0)),
                  pl.BlockSpec((1, d, tn), lambda l, j: (l, 0, j)),
                  pl.BlockSpec((1, 1, tn), lambda l, j: (l, 0, j))],
        out_specs=pl.BlockSpec((1, bsz, tn), lambda l, j: (l, 0, j)),
        out_shape=jax.ShapeDtypeStruct((depth, bsz, n3), F32),
        compiler_params=_cparams(("arbitrary", "arbitrary")),
    )(c, w_ada, b_ada.reshape(depth, 1, n3))


def _modulate(x_ref, mod_ref):
    mod = mod_ref[...]
    shift = mod[:, :D_MODEL]
    scale = mod[:, D_MODEL:2 * D_MODEL]
    return (x_ref[...] * (1.0 + scale) + shift).astype(BF16)


def _rms(x, g, eps=1e-6):
    ms = jnp.mean(x * x, axis=-1, keepdims=True)
    return (x * lax.rsqrt(ms + eps) * g).astype(BF16)


def _proj_mla_kernel(x_ref, mod_ref, w1_ref, gq_ref, gkv_ref, wq1_ref, wq2_ref, wk_ref, wv_ref,
                     ct_ref, st_ref, q_ref, k_ref, v_ref):
    h = _modulate(x_ref, mod_ref)
    pa = _dot(h, w1_ref[...])
    cq = _rms(pa[:, :Q_LORA], gq_ref[...])
    ckv = _rms(pa[:, Q_LORA:Q_LORA + KV_LORA], gkv_ref[...])
    ct = ct_ref[...]
    st = st_ref[...]
    ct6 = jnp.concatenate([ct] * A_HEADS, axis=1)
    st6 = jnp.concatenate([st] * A_HEADS, axis=1)
    qscale = (A_NOPE + A_ROPE) ** -0.5 * LOG2E
    q = (_dot(cq, wq1_ref[...]) * ct6 + _dot(cq, wq2_ref[...]) * st6) * qscale
    q_ref[...] = q.astype(BF16)
    o = Q_LORA + KV_LORA
    kr = pa[:, o:o + LANES] * ct + pa[:, o + LANES:o + 2 * LANES] * st
    k = _dot(ckv, wk_ref[...]) + jnp.concatenate([kr] * A_HEADS, axis=1)
    k_ref[...] = k.astype(BF16)
    v_ref[...] = _dot(ckv, wv_ref[...]).astype(BF16)


def _proj_mla_call(x, mod, w1, gq, gkv, wq1, wq2, wk, wv, ct, st):
    bsz, s_len, d = x.shape
    tm = PROJ_ROWS
    hw = A_HEADS * LANES
    full = lambda a: pl.BlockSpec(a.shape, lambda b, i: (0,) * a.ndim)
    return pl.pallas_call(
        _proj_mla_kernel,
        grid=(bsz, s_len // tm),
        in_specs=[pl.BlockSpec((None, tm, d), lambda b, i: (b, i, 0)),
                  pl.BlockSpec((None, 1, 3 * d), lambda b, i: (b, 0, 0)),
                  full(w1), full(gq), full(gkv), full(wq1), full(wq2), full(wk), full(wv),
                  pl.BlockSpec((tm, LANES), lambda b, i: (i, 0)),
                  pl.BlockSpec((tm, LANES), lambda b, i: (i, 0))],
        out_specs=[pl.BlockSpec((None, tm, hw), lambda b, i: (b, i, 0)),
                   pl.BlockSpec((None, tm, hw), lambda b, i: (b, i, 0)),
                   pl.BlockSpec((None, tm, A_WIDTH), lambda b, i: (b, i, 0))],
        out_shape=[jax.ShapeDtypeStruct((bsz, s_len, hw), BF16),
                   jax.ShapeDtypeStruct((bsz, s_len, hw), BF16),
                   jax.ShapeDtypeStruct((bsz, s_len, A_WIDTH), BF16)],
        compiler_params=_cparams(("parallel", "parallel")),
    )(x, mod, w1, gq, gkv, wq1, wq2, wk, wv, ct, st)


_R_QB, _R_KB, _R_VB = 0, B_WIDTH, 2 * B_WIDTH
_R_QC = 3 * B_WIDTH
_R_KC, _R_VC = _R_QC + C_WIDTH, _R_QC + 2 * C_WIDTH
_R_QI = _R_QC + 3 * C_WIDTH
_R_KI = _R_QI + IDX_HEADS * IDX_DIM
_R_WI = _R_KI + 2 * LANES
_R_GATE = _R_WI + LANES
_R_END = _R_GATE + MIX_WIDTH


def _proj_rest_kernel(x_ref, mod_ref, w2_ref, wvt_ref, qb_ref, kb_ref, vb_ref, qc_ref, kc_ref, vct_ref,
                      qi_ref, ki_ref, wi_ref, gate_ref):
    h = _modulate(x_ref, mod_ref)

    def proj(start, width):
        return _dot(h, w2_ref[:, start:start + width])

    inv_sqrt_dh = HEAD_DIM ** -0.5 * LOG2E
    for ref, val in ((qb_ref, proj(_R_QB, B_WIDTH) * inv_sqrt_dh), (kb_ref, proj(_R_KB, B_WIDTH)),
                     (vb_ref, proj(_R_VB, B_WIDTH))):
        for p in range(B_HEADS // 2):
            ref[p] = val[:, p * LANES:(p + 1) * LANES]
    qc_ref[...] = (proj(_R_QC, C_WIDTH) * inv_sqrt_dh).astype(BF16)
    kc_ref[...] = proj(_R_KC, C_WIDTH).astype(BF16)
    vct_ref[...] = _dot_nt(wvt_ref[...], h).astype(BF16)
    qi_ref[...] = (proj(_R_QI, IDX_HEADS * IDX_DIM) * IDX_DIM ** -0.5).astype(BF16)
    ki_ref[...] = proj(_R_KI, 2 * LANES).astype(BF16)
    wi_ref[...] = proj(_R_WI, LANES) * IDX_HEADS ** -0.5
    gate_ref[...] = proj(_R_GATE, MIX_WIDTH)


def _proj_rest_call(x, mod, w2, wv_t):
    bsz, s_len, d = x.shape
    tm = PROJ_ROWS
    widths = (B_WIDTH, B_WIDTH, B_WIDTH, C_WIDTH, C_WIDTH, C_WIDTH, IDX_HEADS * IDX_DIM,
              2 * LANES, LANES, MIX_WIDTH)
    dtypes = (F32,) * 3 + (BF16,) * 5 + (F32, F32)
    slab = pl.BlockSpec((None, B_HEADS // 2, tm, LANES), lambda b, i: (b, 0, i, 0))
    slab_shape = (bsz, B_HEADS // 2, s_len, LANES)
    return pl.pallas_call(
        _proj_rest_kernel,
        grid=(bsz, s_len // tm),
        in_specs=[pl.BlockSpec((None, tm, d), lambda b, i: (b, i, 0)),
                  pl.BlockSpec((None, 1, 3 * d), lambda b, i: (b, 0, 0)),
                  pl.BlockSpec(w2.shape, lambda b, i: (0, 0)),
                  pl.BlockSpec(wv_t.shape, lambda b, i: (0, 0))],
        out_specs=[slab if n < 3 else pl.BlockSpec((None, C_WIDTH, tm), lambda b, i: (b, 0, i)) if n == 5 else
                   pl.BlockSpec((None, tm, w), lambda b, i: (b, i, 0)) for n, w in enumerate(widths)],
        out_shape=[jax.ShapeDtypeStruct(slab_shape if n < 3 else (bsz, C_WIDTH, s_len) if n == 5 else
                                        (bsz, s_len, w), dt) for n, (w, dt) in enumerate(zip(widths, dtypes))],
        compiler_params=_cparams(("parallel", "parallel")),
    )(x, mod, w2, wv_t)


def _with_ones(v):
    return jnp.concatenate([v, jnp.ones_like(v)], axis=1)


def _init_state(m_ref, acc_ref):
    m_ref[...] = jnp.full(m_ref.shape, MASKED, F32)
    acc_ref[...] = jnp.zeros(acc_ref.shape, F32)


def _pair_output(acc_ref, i0, i1):
    lane = lax.broadcasted_iota(I32, (acc_ref.shape[1], LANES), 1)
    a0, a1 = acc_ref[i0], acc_ref[i1]
    o0 = a0[:, :LANES] / a0[:, LANES:]
    o1 = a1[:, :LANES] / a1[:, LANES:]
    return jnp.where(lane < HEAD_DIM, o0, o1)


def _mla_kernel(q_ref, k_ref, v_ref, o_ref, m_ref, acc_ref):
    qi = pl.program_id(2)
    tq = q_ref.shape[0]
    _init_state(m_ref, acc_ref)
    lane = lax.broadcasted_iota(I32, (tq, 2 * LANES), 1)
    row = lax.broadcasted_iota(I32, (tq, tq), 0)
    col = lax.broadcasted_iota(I32, (tq, tq), 1)

    def step(c, diagonal):
        ks = pl.ds(pl.multiple_of(c * tq, tq), tq)
        k = k_ref[ks, :]
        zero = jnp.zeros_like(k)
        k2 = jnp.concatenate([jnp.where(lane < LANES, k, zero), jnp.where(lane >= LANES, k, zero)], axis=0)
        s2 = _dot_nt(q_ref[...], k2)
        ps, alphas = [], []
        for hh in range(2):
            s = s2[:, hh * tq:(hh + 1) * tq]
            if diagonal:
                s = jnp.where(col <= row, s, MASKED)
            m_prev = m_ref[hh]
            m_new = jnp.maximum(m_prev, jnp.max(s, axis=1, keepdims=True))
            alphas.append(jnp.exp2(m_prev - m_new))
            ps.append(jnp.exp2(s - jnp.tile(m_new, (1, tq // LANES))).astype(BF16))
            m_ref[hh] = m_new
        pv = _dot(jnp.concatenate(ps, axis=0), _with_ones(v_ref[ks, :]))
        for hh in range(2):
            acc_ref[hh] = jnp.tile(alphas[hh], (1, 2)) * acc_ref[hh] + pv[hh * tq:(hh + 1) * tq]

    def body(c, carry):
        step(c, False)
        return carry

    lax.fori_loop(0, qi, body, 0)
    step(qi, True)
    o_ref[...] = _pair_output(acc_ref, 0, 1)


def _mla_call(q, k, v):
    bsz, s_len, _ = q.shape
    tq = MLA_TQ
    npair = A_HEADS // 2
    return pl.pallas_call(
        _mla_kernel,
        grid=(bsz, npair, s_len // tq),
        in_specs=[pl.BlockSpec((None, tq, 2 * LANES), lambda b, p, i: (b, i, p)),
                  pl.BlockSpec((None, s_len, 2 * LANES), lambda b, p, i: (b, 0, p)),
                  pl.BlockSpec((None, s_len, LANES), lambda b, p, i: (b, 0, p))],
        out_specs=pl.BlockSpec((None, tq, LANES), lambda b, p, i: (b, i, p)),
        out_shape=jax.ShapeDtypeStruct((bsz, s_len, A_WIDTH), F32),
        scratch_shapes=[pltpu.VMEM((2, tq, LANES), F32), pltpu.VMEM((2, tq, 2 * LANES), F32)],
        compiler_params=_cparams(("parallel", "parallel", "arbitrary")),
    )(q, k, v)


def _dil_kernel(q_ref, k_ref, v_ref, bias_ref, o_ref, oacc_ref, lse_ref):
    n = DIL_N
    s_len = q_ref.shape[0]
    lane = lax.broadcasted_iota(I32, (2 * n, LANES), 1)
    first = lax.broadcasted_iota(I32, (n, LANES), 1) < HEAD_DIM

    for pat, (_, dil) in enumerate(DILATED_PATTERNS):
        nb = s_len // dil // n
        stride = dil if dil > 1 else None

        def unit(t, pat=pat, dil=dil, nb=nb, stride=stride):
            r, j = t // nb, t % nb
            q_start = r + j * (n * dil)
            b_start = r + jnp.maximum(j - 1, 0) * (n * dil)
            table = jnp.minimum(j, 1)
            q = q_ref[pl.ds(q_start, n, stride=stride), :].astype(BF16)
            k = k_ref[pl.ds(b_start, 2 * n, stride=stride), :].astype(BF16)
            v = _with_ones(v_ref[pl.ds(b_start, 2 * n, stride=stride), :].astype(BF16))
            zero = jnp.zeros_like(k)
            k2 = jnp.concatenate([jnp.where(lane < HEAD_DIM, k, zero), jnp.where(lane >= HEAD_DIM, k, zero)],
                                 axis=0)
            s2 = _dot_nt(q, k2)
            ps, ms = [], []
            for hh in range(2):
                s = s2[:, hh * 2 * n:(hh + 1) * 2 * n] + bias_ref[pat, table, hh]
                m = jnp.max(s, axis=1, keepdims=True)
                ps.append(jnp.exp2(s - m).astype(BF16))
                ms.append(m)
            pv = _dot(jnp.concatenate(ps, axis=0), v)
            outs = [pv[hh * n:(hh + 1) * n, :LANES] / pv[hh * n:(hh + 1) * n, LANES:] for hh in range(2)]
            lses = [ms[hh] + jnp.log2(pv[hh * n:(hh + 1) * n, LANES:]) for hh in range(2)]
            oacc_ref[pat, pl.ds(q_start, n, stride=stride), :] = jnp.where(first, outs[0], outs[1])
            lse_ref[pat, pl.ds(q_start, n, stride=stride), :] = jnp.where(first, lses[0], lses[1])

        def body(i, carry, unit=unit):
            for u in range(DIL_UNROLL):
                unit(i * DIL_UNROLL + u)
            return carry

        lax.fori_loop(0, s_len // n // DIL_UNROLL, body, 0)

    def mix(i, carry):
        rs = pl.ds(pl.multiple_of(i * DIL_MIX_ROWS, DIL_MIX_ROWS), DIL_MIX_ROWS)
        ls = [lse_ref[p, rs, :] for p in range(len(DILATED_PATTERNS))]
        mx = functools.reduce(jnp.maximum, ls)
        es = [jnp.exp2(l - mx) for l in ls]
        num = functools.reduce(lambda a, b: a + b, [e * oacc_ref[p, rs, :] for p, e in enumerate(es)])
        o_ref[rs, :] = num / functools.reduce(lambda a, b: a + b, es)
        return carry

    lax.fori_loop(0, s_len // DIL_MIX_ROWS, mix, 0)


def _dil_bias(slopes_b):
    n = DIL_N
    kj = jnp.arange(2 * n)
    step = jnp.arange(n)[:, None] + n - kj[None, :]
    valid = (step >= 0) & (step <= n)
    tables = []
    for _, dil in DILATED_PATTERNS:
        bias = -(slopes_b * LOG2E)[:, None, None] * (step * dil).astype(F32)[None]
        banded = jnp.where(valid[None], bias, MASKED)
        first = jnp.concatenate([banded[:, :, n:], jnp.full_like(banded[:, :, n:], MASKED)], axis=2)
        tables.append(jnp.stack([first, banded]))
    return jnp.stack(tables).reshape(len(DILATED_PATTERNS), 2, B_HEADS // 2, 2, n, 2 * n)


def _dil_call(q, k, v, slopes_b):
    bsz, npair, s_len, _ = q.shape
    npat = len(DILATED_PATTERNS)
    for window, dil in DILATED_PATTERNS:
        assert window // dil == DIL_N and s_len % (dil * DIL_N * 2) == 0
    bias = _dil_bias(slopes_b)
    slab = pl.BlockSpec((None, None, s_len, LANES), lambda b, p: (b, p, 0, 0))
    return pl.pallas_call(
        _dil_kernel,
        grid=(bsz, npair),
        in_specs=[slab, slab, slab,
                  pl.BlockSpec((npat, 2, None, 2, DIL_N, 2 * DIL_N), lambda b, p: (0, 0, p, 0, 0, 0))],
        out_specs=pl.BlockSpec((None, s_len, LANES), lambda b, p: (b, 0, p)),
        out_shape=jax.ShapeDtypeStruct((bsz, s_len, npair * LANES), F32),
        scratch_shapes=[pltpu.VMEM((npat, s_len, LANES), F32), pltpu.VMEM((npat, s_len, LANES), F32)],
        compiler_params=_cparams(("parallel", "parallel")),
    )(q, k, v, bias)


def _sortable(x):
    b = lax.bitcast_convert_type(x, I32)
    return b ^ ((b >> 31) & 0x7FFFFFFF)


def _unsortable(k):
    return lax.bitcast_convert_type(k ^ ((k >> 31) & 0x7FFFFFFF), F32)


def _dsa_kernel(qi_ref, ki_ref, wi_ref, qc_ref, kc_ref, vt_ref, tri_ref, ztab_ref, wtab_ref, slope_ref, o_ref,
                sc_ref, sel_ref, ist_ref, fst_ref, m_ref, acc_ref):
    qb = pl.program_id(1)
    nq, ch = DSA_Q, DSA_CH
    g = ch // SUBLANES
    q0 = qb * nq
    nk = (q0 + nq + ch - 1) // ch
    key_i = lax.broadcasted_iota(I32, (ch, nq), 0)
    qry_i = lax.broadcasted_iota(I32, (ch, nq), 1)
    rel = key_i - qry_i

    def rows(a):
        return jnp.broadcast_to(a[:1], (SUBLANES, nq))

    def groups(a):
        return a.reshape(g, SUBLANES, nq)

    def gsum(a):
        return jnp.sum(jnp.sum(a.reshape(SUBLANES, g // SUBLANES, SUBLANES, nq), axis=1), axis=0)

    def gmax(a):
        return jnp.max(jnp.max(a.reshape(SUBLANES, g // SUBLANES, SUBLANES, nq), axis=1), axis=0)

    def gmin(a):
        return jnp.min(jnp.min(a.reshape(SUBLANES, g // SUBLANES, SUBLANES, nq), axis=1), axis=0)

    def colsum(a):
        return rows(jnp.sum(gsum(groups(a)), axis=0, keepdims=True))

    w_t = wi_ref[...].T
    w_rows = [w_t[h:h + 1, :] for h in range(IDX_HEADS)]

    def score_body(c, carry):
        s1, s2, n_zero, n_pos = carry
        ks = pl.ds(pl.multiple_of(c * ch, ch), ch)
        k_lo = ki_ref[ks, :LANES]
        k_hi = ki_ref[ks, LANES:]
        sc = jnp.zeros((ch, nq), F32)
        for j in range(IDX_HEADS // 2):
            qj = qi_ref[:, j * LANES:(j + 1) * LANES]
            sc = sc + w_rows[2 * j] * jnp.maximum(_dot_nt(k_lo, qj), 0.0)
            sc = sc + w_rows[2 * j + 1] * jnp.maximum(_dot_nt(k_hi, qj), 0.0)
        causal = rel <= q0 - c * ch
        sc_ref[c] = groups(jnp.where(causal, sc, -jnp.inf))
        kept = jnp.where(causal, sc, 0.0)
        return (s1 + colsum(kept), s2 + colsum(kept * kept),
                n_zero + colsum(jnp.where(causal & (sc >= 0.0), 1.0, 0.0)),
                n_pos + colsum(jnp.where(causal & (sc > 0.0), 1.0, 0.0)))

    zeros = jnp.zeros((SUBLANES, nq), F32)
    s1, s2, c_zero, c_pos = lax.fori_loop(0, nk, score_body, (zeros,) * 4)

    def count(*preds):
        def body(c, accs):
            s = sc_ref[c]
            return tuple(a + gsum(jnp.where(p(s), 1.0, 0.0)) for a, p in zip(accs, preds))
        accs = lax.fori_loop(0, nk, body, (zeros,) * len(preds))
        return tuple(rows(jnp.sum(a, axis=0, keepdims=True)) for a in accs)

    topk = float(TOPK_MAX)
    target = topk + 0.5
    T_KEY, LO, HI = 0, 1, 2
    C_LO, C_HI, REAL_LO, REAL_HI, SIDE, THR, C_THR, DONE, STEP, NEAR = range(10)
    fst_ref[THR] = jnp.full((SUBLANES, nq), -FLT_MAX, F32)
    fst_ref[C_THR] = zeros

    @pl.when(q0 + 1 > TOPK_MAX)
    def _():
        n = (q0 + 1 + lax.broadcasted_iota(I32, (SUBLANES, nq), 1)).astype(F32)
        mu = s1 / n
        sigma = jnp.sqrt(jnp.maximum(s2 / n - mu * mu, 0.0))
        fst_ref[STEP] = wtab_ref[...] * sigma * 1.5
        zero_tie = (c_pos < topk) & (c_zero >= topk)
        above = c_pos >= topk
        below = c_zero < topk
        key0 = jnp.zeros((SUBLANES, nq), I32)
        lo0 = jnp.where(above, key0, jnp.full((SUBLANES, nq), _KEY_NEG_MAX, I32))
        hi0 = jnp.where(below, key0, jnp.full((SUBLANES, nq), _KEY_FIRST_NAN, I32))
        ist_ref[LO] = lo0
        ist_ref[HI] = hi0
        ist_ref[T_KEY] = jnp.clip(_sortable(mu + ztab_ref[...] * sigma), lo0 + 1, hi0 - 1)
        fst_ref[C_LO] = jnp.where(above, c_zero, n)
        fst_ref[C_HI] = jnp.where(below, c_zero, 0.0)
        fst_ref[REAL_LO] = jnp.where(above, 1.0, 0.0)
        fst_ref[REAL_HI] = jnp.where(below, 1.0, 0.0)
        fst_ref[SIDE] = zeros
        fst_ref[NEAR] = zeros
        fst_ref[THR] = jnp.where(zero_tie, 0.0, -FLT_MAX)
        fst_ref[C_THR] = jnp.where(zero_tie, c_zero, 0.0)
        done0 = jnp.where(zero_tie, 1.0, 0.0)
        fst_ref[DONE] = done0

        def cond(st):
            return jnp.logical_and(st[0] < 60, st[1] > 0.0)

        def body(st):
            it = st[0]
            t_key, lo, hi = ist_ref[T_KEY], ist_ref[LO], ist_ref[HI]
            c_lo, c_hi, done, side = fst_ref[C_LO], fst_ref[C_HI], fst_ref[DONE], fst_ref[SIDE]
            t = _unsortable(t_key)
            (c,) = count(lambda s: s >= t[None])
            live = done == 0.0
            hit = live & (jnp.abs(c - topk) <= 1.0)
            up = live & (c > topk + 1.0)
            dn = live & (c < topk - 1.0)
            moved = jnp.where(up, 1.0, jnp.where(dn, -1.0, 0.0))
            repeat = (moved == side) & (moved != 0.0)
            lo = jnp.where(up, t_key, lo)
            c_lo = jnp.where(up, c, c_lo)
            real_lo = jnp.where(up, 1.0, fst_ref[REAL_LO])
            hi = jnp.where(dn, t_key, hi)
            c_hi = jnp.where(dn, c, c_hi)
            real_hi = jnp.where(dn, 1.0, fst_ref[REAL_HI])
            gap = lax.shift_right_logical(hi - lo, 1)
            shut = (gap == 0) & live & jnp.logical_not(hit)
            fst_ref[THR] = jnp.where(hit, t, jnp.where(shut, _unsortable(lo), fst_ref[THR]))
            fst_ref[C_THR] = jnp.where(hit, c, jnp.where(shut, c_lo, fst_ref[C_THR]))
            fst_ref[NEAR] = jnp.where(hit, 1.0, fst_ref[NEAR])
            done = jnp.where(hit | shut, 1.0, done)
            lo_f, hi_f = _unsortable(lo), _unsortable(hi)
            both = real_lo * real_hi
            frac = jnp.where(repeat, 0.5, (c_lo - target) / (c_lo - c_hi))
            inside = lo_f + (hi_f - lo_f) * frac
            guess = jnp.where(both > 0.0, inside, t + fst_ref[STEP] * (c - target))
            g_key = jnp.clip(_sortable(guess), lo + 1, hi - 1)
            late = jnp.where(it >= 14, 1.0, 0.0)
            ist_ref[T_KEY] = jnp.where(late > 0.0, lo + gap, g_key)
            ist_ref[LO] = lo
            ist_ref[HI] = hi
            fst_ref[C_LO] = c_lo
            fst_ref[C_HI] = c_hi
            fst_ref[REAL_LO] = real_lo
            fst_ref[REAL_HI] = real_hi
            fst_ref[SIDE] = jnp.where(moved != 0.0, moved, side)
            fst_ref[DONE] = done
            return it + 1, float(nq) - jnp.sum(done[:1])

        lax.while_loop(cond, body, (jnp.int32(0), float(nq) - jnp.sum(done0[:1])))

        t, c_t, near = fst_ref[THR], fst_ref[C_THR], fst_ref[NEAR]
        short = (near > 0.0) & (c_t == topk - 1.0)
        extra = (near > 0.0) & (c_t == topk + 1.0)
        inf = jnp.full((SUBLANES, nq), jnp.inf, F32)

        def around(c, carry):
            lower, upper = carry
            s = sc_ref[c]
            below = s < t[None]
            return (jnp.maximum(lower, gmax(jnp.where(below, s, -jnp.inf))),
                    jnp.minimum(upper, gmin(jnp.where(below, jnp.inf, s))))

        lower, upper = lax.fori_loop(0, nk, around, (-inf, inf))
        lower = rows(jnp.max(lower, axis=0, keepdims=True))
        upper = rows(jnp.min(upper, axis=0, keepdims=True))
        x = jnp.where(short, lower, jnp.where(extra, upper, t))

        def recount(c, carry):
            n_ge, n_gt, nxt = carry
            s = sc_ref[c]
            gt = s > x[None]
            return (n_ge + gsum(jnp.where(s >= x[None], 1.0, 0.0)), n_gt + gsum(jnp.where(gt, 1.0, 0.0)),
                    jnp.minimum(nxt, gmin(jnp.where(gt, s, jnp.inf))))

        n_ge, n_gt, nxt = lax.fori_loop(0, nk, recount, (zeros, zeros, inf))
        n_ge = rows(jnp.sum(n_ge, axis=0, keepdims=True))
        n_gt = rows(jnp.sum(n_gt, axis=0, keepdims=True))
        nxt = rows(jnp.min(nxt, axis=0, keepdims=True))
        step_up = extra & (n_gt == topk)
        fst_ref[THR] = jnp.where(step_up, nxt, x)
        fst_ref[C_THR] = jnp.where(step_up, n_gt, n_ge)

    thr = fst_ref[THR]
    has_tie = jnp.max(fst_ref[C_THR]) > topk

    @pl.when(jnp.logical_not(has_tie))
    def _():
        def body(c, carry):
            sel_ref[c] = jnp.where(sc_ref[c] >= thr[None], 0.0, MASKED)
            return carry
        lax.fori_loop(0, nk, body, 0)

    @pl.when(has_tie)
    def _():
        (n_gt,) = count(lambda s: s > thr[None])
        need = topk - n_gt

        def body(c, seen):
            s = sc_ref[c]
            eq = s == thr[None]
            ones = jnp.where(eq, 1.0, 0.0).reshape(ch, nq).astype(BF16)
            rank = groups(_dot(tri_ref[...], ones)) + seen[None]
            take = (s > thr[None]) | (eq & (rank <= need[None]))
            sel_ref[c] = jnp.where(take, 0.0, MASKED)
            return rows(rank[g - 1, SUBLANES - 1:SUBLANES, :])
        lax.fori_loop(0, nk, body, zeros)

    m_ref[...] = jnp.full(m_ref.shape, MASKED, F32)
    acc_ref[...] = jnp.zeros(acc_ref.shape, F32)
    lane4 = lax.broadcasted_iota(I32, (ch, C_WIDTH), 1)
    relf = rel.astype(F32)
    ones_rows = jnp.ones((DSA_ONES, ch), BF16)
    ag = acc_ref.shape[1] // SUBLANES

    def attn_body(c, carry):
        ks = pl.ds(pl.multiple_of(c * ch, ch), ch)
        dist = relf + (c * ch - q0).astype(F32)
        sel = sel_ref[c].reshape(ch, nq)
        k = kc_ref[ks, :]
        zero = jnp.zeros_like(k)
        k4 = jnp.concatenate([jnp.where((lane4 >= h * HEAD_DIM) & (lane4 < (h + 1) * HEAD_DIM), k, zero)
                              for h in range(C_HEADS)], axis=0)
        s4 = _dot_nt(k4, qc_ref[...])
        prs, alphas = [], []
        for h in range(C_HEADS):
            s = groups(s4[h * ch:(h + 1) * ch] + (slope_ref[h] * dist + sel))
            m_prev = m_ref[h]
            m_new = jnp.maximum(m_prev, rows(jnp.max(gmax(s), axis=0, keepdims=True)))
            alphas.append(jnp.exp2(m_prev - m_new))
            prs.append(jnp.exp2(s - m_new[None]).reshape(ch, nq).astype(BF16))
            m_ref[h] = m_new
        for p in range(C_HEADS // 2):
            v_t = jnp.concatenate([vt_ref[p * LANES:(p + 1) * LANES, ks], ones_rows], axis=0)
            pv = _dot(v_t, jnp.concatenate(prs[2 * p:2 * p + 2], axis=1))
            for hh in range(2):
                h = 2 * p + hh
                acc = acc_ref[h].reshape(ag, SUBLANES, nq) * alphas[h][None]
                acc_ref[h] = acc.reshape(ag * SUBLANES, nq) + pv[:, hh * nq:(hh + 1) * nq]
        return carry

    lax.fori_loop(0, nk, attn_body, 0)
    dim = lax.broadcasted_iota(I32, (LANES, nq), 0)
    for p in range(C_HEADS // 2):
        outs = []
        for hh in range(2):
            acc = acc_ref[2 * p + hh]
            den = acc[LANES:LANES + SUBLANES]
            outs.append((acc[:LANES].reshape(LANES // SUBLANES, SUBLANES, nq) / den[None]).reshape(LANES, nq))
        o_ref[:, p * LANES:(p + 1) * LANES] = jnp.where(dim < HEAD_DIM, outs[0], outs[1]).T


def _quantile_table(s_len):
    n = jnp.arange(1, s_len + 1, dtype=F32)
    tail = jnp.clip((TOPK_MAX + 0.5) / n, 1e-6, 1.0 - 1e-6)
    z = jax.scipy.special.ndtri(1.0 - tail)
    pdf = jnp.exp(-0.5 * z * z) * (2.0 * np.pi) ** -0.5
    rep = lambda a: jnp.broadcast_to(a[None, :], (SUBLANES, s_len))
    return rep(z), rep(1.0 / (n * pdf))


def _dsa_call(qi, ki, wi, qc, kc, vc_t, slopes_c):
    bsz, s_len, _ = qc.shape
    nq, ch = DSA_Q, DSA_CH
    tri = (jnp.arange(ch)[:, None] >= jnp.arange(ch)[None, :]).astype(BF16)
    ztab, wtab = _quantile_table(s_len)
    blk = lambda w: pl.BlockSpec((None, nq, w), lambda b, i: (b, i, 0))
    seq = lambda w: pl.BlockSpec((None, s_len, w), lambda b, i: (b, 0, 0))
    tab = pl.BlockSpec((SUBLANES, nq), lambda b, i: (0, i))
    return pl.pallas_call(
        _dsa_kernel,
        grid=(bsz, s_len // nq),
        in_specs=[blk(IDX_HEADS * IDX_DIM), seq(2 * LANES), blk(LANES),
                  blk(C_WIDTH), seq(C_WIDTH),
                  pl.BlockSpec((None, C_WIDTH, s_len), lambda b, i: (b, 0, 0)),
                  pl.BlockSpec((ch, ch), lambda b, i: (0, 0)),
                  tab, tab,
                  pl.BlockSpec(memory_space=pltpu.SMEM)],
        out_specs=blk(C_WIDTH),
        out_shape=jax.ShapeDtypeStruct((bsz, s_len, C_WIDTH), F32),
        scratch_shapes=[pltpu.VMEM((s_len // ch, ch // SUBLANES, SUBLANES, nq), F32),
                        pltpu.VMEM((s_len // ch, ch // SUBLANES, SUBLANES, nq), F32),
                        pltpu.VMEM((3, SUBLANES, nq), I32),
                        pltpu.VMEM((10, SUBLANES, nq), F32),
                        pltpu.VMEM((C_HEADS, SUBLANES, nq), F32),
                        pltpu.VMEM((C_HEADS, LANES + DSA_ONES, nq), F32)],
        compiler_params=_cparams(("parallel", "arbitrary")),
    )(qi, ki, wi, qc, kc, vc_t, tri, ztab, wtab, slopes_c * LOG2E)


def _out_kernel(alpha, x_ref, mod_ref, oa_ref, ob_ref, oc_ref, gate_ref, wo_ref, g_ref, b_ref, y_ref):
    pg = gate_ref[...]
    sg = pg * jax.nn.sigmoid(pg)
    ya = (oa_ref[...] * sg[:, :A_WIDTH]).astype(BF16)
    yb = (ob_ref[...] * sg[:, A_WIDTH:A_WIDTH + B_WIDTH]).astype(BF16)
    yc = (oc_ref[...] * sg[:, A_WIDTH + B_WIDTH:]).astype(BF16)
    sub = (_dot(ya, wo_ref[:A_WIDTH, :]) + _dot(yb, wo_ref[A_WIDTH:A_WIDTH + B_WIDTH, :])
           + _dot(yc, wo_ref[A_WIDTH + B_WIDTH:, :]))
    gate = mod_ref[...][:, 2 * D_MODEL:]
    z = alpha * x_ref[...] + (1.0 + gate) * sub
    mu = jnp.mean(z, axis=-1, keepdims=True)
    zc = z - mu
    var = jnp.mean(zc * zc, axis=-1, keepdims=True)
    y_ref[...] = zc * lax.rsqrt(var + 1e-5) * g_ref[...] + b_ref[...]


def _out_call(alpha, x, mod, oa, ob, oc, gate, wo, g, b):
    bsz, s_len, d = x.shape
    tm = PROJ_ROWS
    rows = lambda w: pl.BlockSpec((None, tm, w), lambda bb, i: (bb, i, 0))
    full = lambda a: pl.BlockSpec(a.shape, lambda bb, i: (0,) * a.ndim)
    return pl.pallas_call(
        functools.partial(_out_kernel, alpha),
        grid=(bsz, s_len // tm),
        in_specs=[rows(d), pl.BlockSpec((None, 1, 3 * d), lambda bb, i: (bb, 0, 0)),
                  rows(A_WIDTH), rows(B_WIDTH), rows(C_WIDTH), rows(MIX_WIDTH),
                  full(wo), full(g), full(b)],
        out_specs=rows(d),
        out_shape=jax.ShapeDtypeStruct((bsz, s_len, d), F32),
        compiler_params=_cparams(("parallel", "parallel")),
    )(x, mod, oa, ob, oc, gate, wo, g, b)


def _pad_cols(a, width):
    return jnp.pad(a, ((0, 0), (0, width - a.shape[1])))


def _swap_halves(a):
    half = a.shape[1] // 2
    return jnp.concatenate([a[:, half:], a[:, :half]], axis=1)


def _layer_weights(w_in, w_uq, w_uk, w_uv, w_out):
    d = w_in.shape[0]
    z = lambda n: jnp.zeros((d, n), w_in.dtype)
    w_kr = w_in[:, O_KR:O_QB]
    w1 = jnp.concatenate([w_in[:, O_CQ:O_KR],
                          z(A_NOPE), w_kr, z(LANES - A_NOPE - A_ROPE),
                          z(A_NOPE), _swap_halves(w_kr), z(LANES - A_NOPE - A_ROPE)], axis=1)
    hq = A_NOPE + A_ROPE
    q1, q2 = [], []
    for h in range(A_HEADS):
        blk = w_uq[:, h * hq:(h + 1) * hq]
        q1.append(_pad_cols(blk, LANES))
        zq = jnp.zeros((w_uq.shape[0], A_NOPE), w_uq.dtype)
        q2.append(_pad_cols(jnp.concatenate([zq, _swap_halves(blk[:, A_NOPE:])], axis=1), LANES))
    wq1 = jnp.concatenate(q1, axis=1)
    wq2 = jnp.concatenate(q2, axis=1)
    wk = jnp.concatenate([_pad_cols(w_uk[:, h * A_NOPE:(h + 1) * A_NOPE], LANES)
                          for h in range(A_HEADS)], axis=1)
    w_ki = w_in[:, O_KI:O_WI]
    w2 = jnp.concatenate([w_in[:, O_QB:O_KI],
                          w_ki, z(LANES - IDX_DIM), z(LANES - IDX_DIM), w_ki,
                          w_in[:, O_WI:O_GATE], z(LANES - IDX_HEADS),
                          w_in[:, O_GATE:]], axis=1)
    bf = lambda a: a.astype(BF16)
    wv_t = w_in[:, O_VC:O_QI].T
    return bf(w1), bf(wq1), bf(wq2), bf(wk), bf(w_uv), bf(w2), bf(wv_t), bf(w_out)


def _rope_tables(s_len):
    pos = jnp.arange(s_len, dtype=F32)
    freqs = ROPE_THETA ** (-jnp.arange(0, A_ROPE, 2, dtype=F32) / A_ROPE)
    ang = pos[:, None] * freqs[None, :]
    cos, sin = jnp.cos(ang), jnp.sin(ang)
    ones = jnp.ones((s_len, A_NOPE), F32)
    zeros = jnp.zeros((s_len, LANES - A_NOPE - A_ROPE), F32)
    ct = jnp.concatenate([ones, cos, cos, zeros], axis=1)
    st = jnp.concatenate([jnp.zeros((s_len, A_NOPE), F32), -sin, sin, zeros], axis=1)
    return ct, st


def kernel(x, c, w_ada, b_ada, w_in, q_norm_g, kv_norm_g, w_uq, w_uk, w_uv, w_out, ln_g, ln_b):
    bsz, s_len, d = x.shape
    depth = w_ada.shape[0]
    alpha = (2 * depth) ** 0.25
    slopes = 2.0 ** (-8.0 * jnp.arange(1, N_ALIBI + 1, dtype=F32) / N_ALIBI)
    slopes_b, slopes_c = slopes[:B_HEADS], slopes[B_HEADS:]
    ct, st = _rope_tables(s_len)
    mod_all = _mod_call(c, w_ada, b_ada)

    for l in range(depth):
        w1, wq1, wq2, wk, wv, w2, wv_t, wo = _layer_weights(w_in[l], w_uq[l], w_uk[l], w_uv[l], w_out[l])
        mod = mod_all[l].reshape(bsz, 1, 3 * d)
        qa, ka, va = _proj_mla_call(x, mod, w1, q_norm_g[l].reshape(1, -1), kv_norm_g[l].reshape(1, -1),
                                    wq1, wq2, wk, wv, ct, st)
        qb, kb, vb, qc, kc, vc_t, qi, ki, wi, gate = _proj_rest_call(x, mod, w2, wv_t)
        o_a = _mla_call(qa, ka, va)
        o_b = _dil_call(qb, kb, vb, slopes_b)
        o_c = _dsa_call(qi, ki, wi, qc, kc, vc_t, slopes_c)
        x = _out_call(alpha, x, mod, o_a, o_b, o_c, gate, wo,
                      ln_g[l].reshape(1, -1), ln_b[l].reshape(1, -1))
    return x
```
